```python
import jax, jax.numpy as jnp
from jax import lax
import numpy as np

D_MODEL = 1024
BATCH = 2
SEQ = 8192
DEPTH = 4

GRID_W = 64
CTX_LEN = 256
HEAD_DIM = 64
RWKV_HEADS = 6
GQA_Q_HEADS = 6
GQA_KV_HEADS = 2
GQA_REP = GQA_Q_HEADS // GQA_KV_HEADS
NA_HEADS = 4
RWKV_W = RWKV_HEADS * HEAD_DIM
GQA_W = GQA_Q_HEADS * HEAD_DIM
GQA_KV_W = GQA_KV_HEADS * HEAD_DIM
NA_W = NA_HEADS * HEAD_DIM
D_MIX = RWKV_W + GQA_W + NA_W
DECAY_RANK = 32
ICLR_RANK = 32
RWKV_CONV_W = 3 * RWKV_W + 2 * DECAY_RANK + 2 * ICLR_RANK
RWKV_SPLIT = (RWKV_W, 2 * RWKV_W, 3 * RWKV_W, 3 * RWKV_W + 2 * DECAY_RANK)
_SPLIT_SIZES = (RWKV_CONV_W, RWKV_W, GQA_W, GQA_KV_W, GQA_KV_W, GQA_W, NA_W, NA_W, NA_W, NA_W)
D_IN = sum(_SPLIT_SIZES)
SPLIT_POINTS = tuple(int(s) for s in np.cumsum(_SPLIT_SIZES)[:-1])
Q_BLOCK = 128
NA_WIN_ROWS = 8
NA_WIN_COLS = 16
ROPE_THETA = 10000.0
DEEPNORM_ALPHA = (2 * DEPTH) ** 0.25
DEEPNORM_BETA = (8 * DEPTH) ** -0.25
LN_EPS = 1e-5
RMS_EPS = 1e-6
GN_EPS = 64e-5

kernel_name = 'hymba_style_rwkv7_gqa_natten_dit'


def _layer_norm(x, g, b):
    xf = x.astype(jnp.float32)
    mu = jnp.mean(xf, axis=-1, keepdims=True)
    var = jnp.mean(jnp.square(xf - mu), axis=-1, keepdims=True)
    return ((xf - mu) * lax.rsqrt(var + LN_EPS) * g + b).astype(x.dtype)


def _rms_norm(x, g):
    xf = x.astype(jnp.float32)
    return (xf * lax.rsqrt(jnp.mean(xf * xf, axis=-1, keepdims=True) + RMS_EPS) * g).astype(x.dtype)


def _centred_conv3(x, w):
    xp = jnp.pad(x, ((0, 0), (1, 1), (0, 0)))
    return w[0] * xp[:, :-2] + w[1] * xp[:, 1:-1] + w[2] * xp[:, 2:]


def _axial_rope_tables(n_tokens):
    t = jnp.arange(n_tokens, dtype=jnp.int32)
    row = (t // GRID_W).astype(jnp.float32)
    col = (t % GRID_W).astype(jnp.float32)
    n_freq = HEAD_DIM // 4
    inv_freq = ROPE_THETA ** (-jnp.arange(n_freq, dtype=jnp.float32) / n_freq)
    ang_r = row[:, None] * inv_freq
    ang_c = col[:, None] * inv_freq
    ang = jnp.concatenate([ang_r, ang_r, ang_c, ang_c], axis=-1)
    return jnp.cos(ang), jnp.sin(ang)


def _apply_axial_rope(x, cos, sin):
    x1, x2, x3, x4 = jnp.split(x, 4, axis=-1)
    rotated = jnp.concatenate([-x2, x1, -x4, x3], axis=-1)
    return (x * cos[:, None] + rotated * sin[:, None]).astype(x.dtype)


def _rwkv_inputs(u, conv_w, decay_w0, decay_w2, iclr_a0, iclr_a2, k_k, k_a):
    u = _centred_conv3(u, conv_w).astype(jnp.float32)
    B, T, _ = u.shape
    r, k, v, dd, ad = jnp.split(u, RWKV_SPLIT, axis=-1)
    dd = jnp.tanh(dd.reshape(B, T, 2, DECAY_RANK))
    ad = ad.reshape(B, T, 2, ICLR_RANK)
    w_log = -jax.nn.softplus(-(decay_w0 + jnp.einsum('btdr,drc->btdc', dd, decay_w2))) - 0.5
    decay = jnp.exp(-jnp.exp(w_log))
    iclr = jax.nn.sigmoid(iclr_a0 + jnp.einsum('btdr,drc->btdc', ad, iclr_a2))
    kk = (k * k_k).reshape(B, T, RWKV_HEADS, HEAD_DIM)
    kk = kk * lax.rsqrt(jnp.maximum(jnp.sum(kk * kk, axis=-1, keepdims=True), 1e-12))
    kk = kk.reshape(B, T, RWKV_W)
    kd = k[:, :, None] * (1.0 + (iclr - 1.0) * k_a)
    bd = kk[:, :, None] * iclr

    def heads_t(z):
        return jnp.moveaxis(z.reshape(z.shape[:-1] + (RWKV_HEADS, HEAD_DIM)), 1, 0)

    return (heads_t(r), heads_t(decay), heads_t(kd), heads_t(v), heads_t(kk), heads_t(bd))


def _rwkv7_step(S, inp):
    r, w, k, v, kk, b = inp
    sa = jnp.einsum('bhvk,bhk->bhv', S, kk)
    S = S * w[:, :, None, :] - sa[..., None] * b[:, :, None, :] + v[..., None] * k[:, :, None, :]
    return S, jnp.einsum('bhvk,bhk->bhv', S, r)


def _rwkv_bidir(rin, S0_f, S0_b):
    r, decay, kd, v, kk, bd = rin
    S_f, y_f = lax.scan(_rwkv7_step, S0_f, (r, decay[:, :, 0], kd[:, :, 0], v, kk, bd[:, :, 0]))
    S_b, y_b = lax.scan(_rwkv7_step, S0_b, (r, decay[:, :, 1], kd[:, :, 1], v, kk, bd[:, :, 1]),
                        reverse=True)
    return S_f, S_b, y_f + y_b


def _rwkv_output(ys, rin, r_k, gn_g, gn_b, dtype):
    r, _, kd, v, _, _ = rin
    T, B = ys.shape[:2]
    mu = jnp.mean(ys, axis=-1, keepdims=True)
    var = jnp.mean(jnp.square(ys - mu), axis=-1, keepdims=True)
    yn = ((ys - mu) * lax.rsqrt(var + GN_EPS)).reshape(T, B, RWKV_W) * gn_g + gn_b
    bonus = jnp.sum(jnp.sum(r[:, :, None] * kd * r_k, axis=-1, keepdims=True), axis=2) * v
    return jnp.moveaxis(yn + bonus.reshape(T, B, RWKV_W), 0, 1).astype(dtype)


def _block_attention(q, keys, vals):
    B, T, G, R, D = q.shape
    qb = jnp.moveaxis(q.reshape(B, T // Q_BLOCK, Q_BLOCK, G, R, D), 1, 0) * (D ** -0.5)

    def one_block(qi):
        s = jnp.einsum('bqgrd,bsgd->bgrqs', qi, keys).astype(jnp.float32)
        p = jax.nn.softmax(s, axis=-1).astype(vals.dtype)
        return jnp.einsum('bgrqs,bsgd->bqgrd', p, vals)

    o = lax.map(one_block, qb)
    return jnp.moveaxis(o, 0, 1).reshape(B, T, G * R * D)


def _neighbourhood_attention(q, k, v, k_ctx, v_ctx, rpb):
    B, L, H, D = q.shape
    rows = L // GRID_W
    win_r = min(NA_WIN_ROWS, rows)
    n_nb = win_r * NA_WIN_COLS
    qg = q.reshape(B, rows, GRID_W, H, D) * (D ** -0.5)
    kg = k.reshape(B, rows, GRID_W, H, D)
    vg = v.reshape(B, rows, GRID_W, H, D)
    cols = jnp.arange(GRID_W, dtype=jnp.int32)
    c_start = jnp.clip(cols - NA_WIN_COLS // 2, 0, GRID_W - NA_WIN_COLS)
    c_idx = c_start[:, None] + jnp.arange(NA_WIN_COLS, dtype=jnp.int32)[None, :]
    dc = c_idx - cols[:, None] + (NA_WIN_COLS - 1)

    def row_block(r):
        r_start = jnp.clip(r - win_r // 2, 0, rows - win_r)
        q_r = lax.dynamic_index_in_dim(qg, r, axis=1, keepdims=False)
        k_nb = lax.dynamic_slice_in_dim(kg, r_start, win_r, axis=1)[:, :, c_idx]
        v_nb = lax.dynamic_slice_in_dim(vg, r_start, win_r, axis=1)[:, :, c_idx]
        dr = r_start + jnp.arange(win_r, dtype=jnp.int32) - r + (NA_WIN_ROWS - 1)
        bias = rpb[:, dr[None, :, None], dc[:, None, :]]
        s_nb = jnp.einsum('bqhd,bpqjhd->bhqpj', q_r, k_nb).astype(jnp.float32) + bias
        s_ctx = jnp.einsum('bqhd,bchd->bhqc', q_r, k_ctx).astype(jnp.float32)
        s = jnp.concatenate([s_nb.reshape(B, H, GRID_W, n_nb), s_ctx], axis=-1)
        p = jax.nn.softmax(s, axis=-1).astype(v.dtype)
        p_nb = p[..., :n_nb].reshape(B, H, GRID_W, win_r, NA_WIN_COLS)
        return (jnp.einsum('bhqpj,bpqjhd->bqhd', p_nb, v_nb)
                + jnp.einsum('bhqc,bchd->bqhd', p[..., n_nb:], v_ctx))

    o = lax.map(row_block, jnp.arange(rows, dtype=jnp.int32))
    return jnp.moveaxis(o, 0, 1).reshape(B, L, H * D)


def _layer(x, ctx, c, c_ctx, rope, w_mod, b_mod, w_in, w_out, rwkv_conv, decay_w0, decay_w2,
           iclr_a0, iclr_a2, k_k, k_a, r_k, gn_g, gn_b, q_norm, k_norm, rpb, ln_g, ln_b, update_ctx):
    B, L, _ = x.shape
    cos, sin = rope
    shift, scale, gate = jnp.split(jax.nn.silu(c) @ w_mod + b_mod, 3, axis=-1)
    shift_c, scale_c, gate_c = jnp.split(jax.nn.silu(c_ctx) @ w_mod + b_mod, 3, axis=-1)
    h_lat = x * (1.0 + scale[:, None]) + shift[:, None]
    h_ctx = ctx * (1.0 + scale_c) + shift_c
    ru_l, rg_l, aq_l, ak_l, av_l, ag_l, nq_l, nk_l, nv_l, ng_l = jnp.split(h_lat @ w_in, SPLIT_POINTS, axis=-1)
    ru_c, rg_c, aq_c, ak_c, av_c, ag_c, nq_c, nk_c, nv_c, ng_c = jnp.split(h_ctx @ w_in, SPLIT_POINTS, axis=-1)

    rwkv_params = (rwkv_conv, decay_w0, decay_w2, iclr_a0, iclr_a2, k_k, k_a)
    rin_c = _rwkv_inputs(ru_c, *rwkv_params)
    rin_l = _rwkv_inputs(ru_l, *rwkv_params)
    S0 = jnp.zeros((B, RWKV_HEADS, HEAD_DIM, HEAD_DIM), jnp.float32)
    S_f, S_b, ys_c = _rwkv_bidir(rin_c, S0, S0)
    _, _, ys_l = _rwkv_bidir(rin_l, S_f, S_b)
    yr_l = _rwkv_output(ys_l, rin_l, r_k, gn_g, gn_b, x.dtype)

    def gqa_kv(kx, vx):
        T = kx.shape[1]
        return (_rms_norm(kx.reshape(B, T, GQA_KV_HEADS, HEAD_DIM), k_norm),
                vx.reshape(B, T, GQA_KV_HEADS, HEAD_DIM))

    def gqa_q(qx):
        return _rms_norm(qx.reshape(B, qx.shape[1], GQA_Q_HEADS, HEAD_DIM), q_norm)

    gk_l, gv_l = gqa_kv(ak_l, av_l)
    gk_l = _apply_axial_rope(gk_l, cos, sin)
    gq_l = _apply_axial_rope(gqa_q(aq_l), cos, sin)
    gk_c, gv_c = gqa_kv(ak_c, av_c)
    ya_l = _block_attention(gq_l.reshape(B, L, GQA_KV_HEADS, GQA_REP, HEAD_DIM),
                            jnp.concatenate([gk_c, gk_l], axis=1), jnp.concatenate([gv_c, gv_l], axis=1))

    def na_heads(u):
        return u.reshape(B, u.shape[1], NA_HEADS, HEAD_DIM)

    nk_c, nv_c = na_heads(nk_c), na_heads(nv_c)
    yn_l = _neighbourhood_attention(na_heads(nq_l), na_heads(nk_l), na_heads(nv_l), nk_c, nv_c, rpb)

    y_l = jnp.concatenate([yr_l * jax.nn.silu(rg_l), ya_l * jax.nn.silu(ag_l),
                           yn_l * jax.nn.silu(ng_l)], axis=-1) @ w_out
    x = _layer_norm(DEEPNORM_ALPHA * x + gate[:, None] * y_l, ln_g, ln_b)

    if update_ctx:
        n_ctx = ctx.shape[1]
        yr_c = _rwkv_output(ys_c, rin_c, r_k, gn_g, gn_b, ctx.dtype)
        ya_c = _block_attention(gqa_q(aq_c).reshape(B, n_ctx, GQA_KV_HEADS, GQA_REP, HEAD_DIM), gk_c, gv_c)
        yn_c = _block_attention(na_heads(nq_c)[:, :, :, None], nk_c, nv_c)
        y_c = jnp.concatenate([yr_c * jax.nn.silu(rg_c), ya_c * jax.nn.silu(ag_c),
                               yn_c * jax.nn.silu(ng_c)], axis=-1) @ w_out
        ctx = _layer_norm(DEEPNORM_ALPHA * ctx + gate_c * y_c, ln_g, ln_b)
    else:
        ctx = None
    return x, ctx


def setup_inputs(seed: int = 0) -> dict:
    key = jax.random.key(seed)
    ks = jax.random.split(key, 24)

    def n(k, shape):
        return jax.random.normal(k, shape, jnp.float32)

    inv_d = D_MODEL ** -0.5
    return {
        'x': n(ks[0], (BATCH, SEQ, D_MODEL)),
        'c': n(ks[1], (BATCH, D_MODEL)),
        'ctx': n(ks[2], (BATCH, CTX_LEN, D_MODEL)),
        'c_ctx': n(ks[3], (D_MODEL,)),
        'w_mod': n(ks[4], (DEPTH, D_MODEL, 3 * D_MODEL)) * (0.5 * inv_d),
        'b_mod': 0.01 * n(ks[5], (DEPTH, 3 * D_MODEL)),
        'w_in': n(ks[6], (DEPTH, D_MODEL, D_IN)) * inv_d,
        'w_out': n(ks[7], (DEPTH, D_MIX, D_MODEL)) * (DEEPNORM_BETA * D_MIX ** -0.5),
        'rwkv_conv': jnp.array([0.25, 0.5, 0.25], jnp.float32)[None, :, None]
                     + 0.05 * n(ks[8], (DEPTH, 3, RWKV_CONV_W)),
        'decay_w0': n(ks[9], (DEPTH, 2, RWKV_W)),
        'decay_w2': 0.1 * n(ks[10], (DEPTH, 2, DECAY_RANK, RWKV_W)),
        'iclr_a0': 0.5 * n(ks[11], (DEPTH, 2, RWKV_W)),
        'iclr_a2': 0.1 * n(ks[12], (DEPTH, 2, ICLR_RANK, RWKV_W)),
        'rwkv_k_k': 0.85 + 0.05 * n(ks[13], (DEPTH, RWKV_W)),
        'rwkv_k_a': 1.0 + 0.05 * n(ks[14], (DEPTH, RWKV_W)),
        'rwkv_r_k': 0.1 * n(ks[15], (DEPTH, RWKV_HEADS, HEAD_DIM)),
        'rwkv_gn_g': 1.0 + 0.05 * n(ks[16], (DEPTH, RWKV_W)),
        'rwkv_gn_b': 0.01 * n(ks[17], (DEPTH, RWKV_W)),
        'gqa_q_norm': 1.0 + 0.05 * n(ks[18], (DEPTH, HEAD_DIM)),
        'gqa_k_norm': 1.0 + 0.05 * n(ks[19], (DEPTH, HEAD_DIM)),
        'na_rpb': 0.1 * n(ks[20], (DEPTH, NA_HEADS, 2 * NA_WIN_ROWS - 1, 2 * NA_WIN_COLS - 1)),
        'ln_g': 1.0 + 0.05 * n(ks[21], (DEPTH, D_MODEL)),
        'ln_b': 0.01 * n(ks[22], (DEPTH, D_MODEL)),
    }


def reference(x, c, ctx, c_ctx, w_mod, b_mod, w_in, w_out, rwkv_conv, decay_w0, decay_w2,
              iclr_a0, iclr_a2, rwkv_k_k, rwkv_k_a, rwkv_r_k, rwkv_gn_g, rwkv_gn_b,
              gqa_q_norm, gqa_k_norm, na_rpb, ln_g, ln_b):
    rope = _axial_rope_tables(x.shape[1])
    for l in range(DEPTH):
        x, ctx = _layer(x, ctx, c, c_ctx, rope, w_mod[l], b_mod[l], w_in[l], w_out[l], rwkv_conv[l],
                        decay_w0[l], decay_w2[l], iclr_a0[l], iclr_a2[l], rwkv_k_k[l], rwkv_k_a[l],
                        rwkv_r_k[l], rwkv_gn_g[l], rwkv_gn_b[l], gqa_q_norm[l], gqa_k_norm[l],
                        na_rpb[l], ln_g[l], ln_b[l], l < DEPTH - 1)
    return x
```

```python
import functools

import numpy as np
import jax
import jax.numpy as jnp
from jax import lax
from jax.experimental import pallas as pl
from jax.experimental.pallas import tpu as pltpu

F32 = jnp.float32
BF16 = jnp.bfloat16
HI = lax.Precision.HIGHEST

HEAD_DIM = 64
RWKV_HEADS = 6
GQA_Q_HEADS = 6
GQA_KV_HEADS = 2
NA_HEADS = 4
RWKV_W = RWKV_HEADS * HEAD_DIM
GQA_W = GQA_Q_HEADS * HEAD_DIM
GQA_KV_W = GQA_KV_HEADS * HEAD_DIM
NA_W = NA_HEADS * HEAD_DIM
LOW_RANK = 32
RWKV_CONV_W = 3 * RWKV_W + 4 * LOW_RANK
GRID_W = 64
NA_WIN_ROWS = 8
NA_WIN_COLS = 16
ROPE_THETA = 10000.0
LN_EPS = 1e-5
RMS_EPS = 1e-6
GN_EPS = 64e-5
KK_EPS = 1e-12
MASK_BIAS = -1e30

LANES = 128
SUBLANES = 8
TOKEN_TILE = 256
SCAN_CHUNK = 64
KV_CHUNK = 256
VMEM_LIMIT = 56 * 1024 * 1024


def _cparams(*sem):
    return pltpu.CompilerParams(dimension_semantics=sem, vmem_limit_bytes=VMEM_LIMIT)


def _dot(a, b, precision=None):
    return jnp.dot(a, b, preferred_element_type=F32, precision=precision)


def _dot_nt(a, b, precision=None):
    return lax.dot_general(a, b, (((1,), (1,)), ((), ())), preferred_element_type=F32, precision=precision)


def _dot_tn(a, b, precision=None):
    return lax.dot_general(a, b, (((0,), (0,)), ((), ())), preferred_element_type=F32, precision=precision)


def _sigmoid(x):
    return 1.0 / (1.0 + jnp.exp(-x))


def _silu(x):
    return x * _sigmoid(x)


def _mod_kernel(c_ref, w_ref, b_ref, o_ref):
    o_ref[...] = _dot(_silu(c_ref[...]), w_ref[...], HI) + b_ref[...]


def _modulation(cvec, w_mod, b_mod):
    depth, d, d3 = w_mod.shape
    r = cvec.shape[0]
    nj = d3 // d
    return pl.pallas_call(
        _mod_kernel,
        grid=(depth, nj),
        in_specs=[
            pl.BlockSpec((r, d), lambda l, j: (0, 0)),
            pl.BlockSpec((None, d, d), lambda l, j: (l, 0, j)),
            pl.BlockSpec((None, 1, d), lambda l, j: (l, 0, j)),
        ],
        out_specs=pl.BlockSpec((None, r, d), lambda l, j: (l, 0, j)),
        out_shape=jax.ShapeDtypeStruct((depth, r, d3), F32),
        compiler_params=_cparams("parallel", "parallel"),
        name="modulation",
    )(cvec, w_mod, b_mod.reshape(depth, 1, d3))


_PROJ_SLABS = (
    ("ru", RWKV_CONV_W, F32),
    ("aq", GQA_W, F32),
    ("ak", GQA_KV_W, F32),
    ("av", GQA_KV_W, BF16),
    ("nq", NA_W, None),
    ("nk", NA_W, BF16),
    ("nv", NA_W, BF16),
    ("gates", RWKV_W + GQA_W + NA_W, BF16),
)


def _proj_kernel(x_ref, mod_ref, w_ref, ru_ref, aq_ref, ak_ref, av_ref, nq_ref, nk_ref, nv_ref, g_ref):
    shift = mod_ref[0:1, :]
    scale = mod_ref[1:2, :]
    h = (x_ref[...] * (1.0 + scale) + shift).astype(BF16)
    outs = {"ru": ru_ref, "aq": aq_ref, "ak": ak_ref, "av": av_ref, "nk": nk_ref, "nv": nv_ref, "gates": g_ref}
    off = 0
    for name, width, _ in _PROJ_SLABS:
        u = _dot(h, w_ref[:, off:off + width])
        if name == "nq":
            u = u * (HEAD_DIM ** -0.5)
            lane = lax.broadcasted_iota(jnp.int32, (u.shape[0], LANES), 1)
            for hd in range(NA_HEADS):
                pair = u[:, (hd // 2) * LANES:(hd // 2 + 1) * LANES]
                keep = (lane < HEAD_DIM) if hd % 2 == 0 else (lane >= HEAD_DIM)
                nq_ref[:, hd * LANES:(hd + 1) * LANES] = jnp.where(keep, pair, 0.0).astype(BF16)
        else:
            outs[name][...] = u.astype(outs[name].dtype)
        off += width


def _projection(xc, modsel, w_perm, n_ctx_tiles):
    b, t, d = xc.shape
    nt = t // TOKEN_TILE
    n_in = w_perm.shape[1]
    shapes, specs = [], []
    for name, width, dt in _PROJ_SLABS:
        if name == "nq":
            width, dt = NA_HEADS * LANES, BF16
        shapes.append(jax.ShapeDtypeStruct((b, t, width), dt))
        specs.append(pl.BlockSpec((None, TOKEN_TILE, width), lambda bi, i: (bi, i, 0)))
    return pl.pallas_call(
        _proj_kernel,
        grid=(b, nt),
        in_specs=[
            pl.BlockSpec((None, TOKEN_TILE, d), lambda bi, i: (bi, i, 0)),
            pl.BlockSpec((None, None, 3, d), lambda bi, i: (bi, jnp.where(i < n_ctx_tiles, 0, 1), 0, 0)),
            pl.BlockSpec((d, n_in), lambda bi, i: (0, 0)),
        ],
        out_specs=specs,
        out_shape=shapes,
        compiler_params=_cparams("parallel", "parallel"),
        name="proj",
    )(xc, modsel, w_perm)


def _rwkv_prep_kernel(x_ref, xp_ref, xn_ref, cw_ref, wlr_ref, b0_ref, kk_ref, ka_ref, rk_ref, ones_ref,
                      r_o, v_o, kk_o, lw0_o, kd0_o, bd0_o, lw1_o, kd1_o, bd1_o, bonus_o, pad_ref, *, n_ctx_tiles):
    i = pl.program_id(1)
    nt = pl.num_programs(1)
    tm = x_ref.shape[0]
    prev_ok = jnp.logical_and(i != 0, i != n_ctx_tiles)
    next_ok = jnp.logical_and(i != n_ctx_tiles - 1, i != nt - 1)
    pad_ref[0:SUBLANES, :] = jnp.where(prev_ok, xp_ref[...], 0.0)
    pad_ref[SUBLANES:SUBLANES + tm, :] = x_ref[...]
    pad_ref[SUBLANES + tm:2 * SUBLANES + tm, :] = jnp.where(next_ok, xn_ref[...], 0.0)
    x_prev = pad_ref[pl.ds(SUBLANES - 1, tm), :]
    x_next = pad_ref[pl.ds(SUBLANES + 1, tm), :]
    u = cw_ref[0:1, :] * x_prev + cw_ref[1:2, :] * x_ref[...] + cw_ref[2:3, :] * x_next

    w = RWKV_W
    r, k, v = u[:, 0:w], u[:, w:2 * w], u[:, 2 * w:3 * w]
    lr = u[:, 3 * w:3 * w + LANES]
    lane = lax.broadcasted_iota(jnp.int32, lr.shape, 1)
    z = jnp.where(lane < 2 * LOW_RANK, jnp.tanh(lr), lr)
    pre = _dot(z, wlr_ref[...], HI) + b0_ref[...]

    ones = ones_ref[...]
    kk0 = k * kk_ref[...]
    ss = _dot(kk0 * kk0, ones, HI)
    kk = kk0 * lax.rsqrt(jnp.maximum(ss, KK_EPS))
    r_o[...] = r
    v_o[...] = v
    kk_o[...] = kk
    kd_sum = None
    for d, (lw_o, kd_o, bd_o) in enumerate(((lw0_o, kd0_o, bd0_o), (lw1_o, kd1_o, bd1_o))):
        xd = -pre[:, d * w:(d + 1) * w]
        softplus = jnp.maximum(xd, 0.0) + jnp.log(1.0 + jnp.exp(-jnp.abs(xd)))
        lw_o[...] = -jnp.exp(-softplus - 0.5)
        iclr = _sigmoid(pre[:, (2 + d) * w:(3 + d) * w])
        kd = k * (1.0 + (iclr - 1.0) * ka_ref[...])
        kd_o[...] = kd
        bd_o[...] = kk * iclr
        kd_sum = kd if kd_sum is None else kd_sum + kd
    bonus_o[...] = _dot(r * kd_sum * rk_ref[...], ones, HI) * v


def _rwkv_prep(ru, conv_w, wlr, b0, k_k, k_a, r_k, ones_heads, n_ctx_tiles):
    b, t, cw = ru.shape
    nt = t // TOKEN_TILE
    hb = TOKEN_TILE // SUBLANES
    w = RWKV_W
    full = lambda shape: pl.BlockSpec(shape, lambda bi, i: (0,) * len(shape))
    out_spec = pl.BlockSpec((None, TOKEN_TILE, w), lambda bi, i: (bi, i, 0))
    return pl.pallas_call(
        functools.partial(_rwkv_prep_kernel, n_ctx_tiles=n_ctx_tiles),
        grid=(b, nt),
        in_specs=[
            pl.BlockSpec((None, TOKEN_TILE, cw), lambda bi, i: (bi, i, 0)),
            pl.BlockSpec((None, SUBLANES, cw), lambda bi, i: (bi, jnp.maximum(i * hb - 1, 0), 0)),
            pl.BlockSpec((None, SUBLANES, cw), lambda bi, i: (bi, jnp.minimum((i + 1) * hb, t // SUBLANES - 1), 0)),
            full((3, cw)), full((LANES, 4 * w)), full((1, 4 * w)),
            full((1, w)), full((1, w)), full((1, w)), full((w, w)),
        ],
        out_specs=[out_spec] * 10,
        out_shape=[jax.ShapeDtypeStruct((b, t, w), F32)] * 10,
        scratch_shapes=[pltpu.VMEM((TOKEN_TILE + 2 * SUBLANES, cw), F32)],
        compiler_params=_cparams("parallel", "parallel"),
        name="rwkv_prep",
    )(ru, ru, ru, conv_w, wlr, b0, k_k, k_a, r_k, ones_heads)


def _chunk_step(r, k, v, kap, bet, lw, h0, reverse, consts):
    tri, strict, incl, head_rows, eye2, eye_l = consts
    c = r.shape[0]
    lc = _dot(tri, lw, HI)
    ltot = lc[0:1, :] if reverse else lc[c - 1:c, :]
    e_in = jnp.exp(lc)
    e_inv = jnp.exp(-lc)
    e_hat = jnp.exp(ltot - lc)

    def stack(x):
        return jnp.where(head_rows, jnp.concatenate([x, x], axis=0), 0.0)

    r_s = stack(r * e_in)
    kap_s = stack(kap * jnp.exp(lc - lw))
    k_s = stack(k * e_inv)
    b_s = stack(bet * e_inv)
    kh_s = stack(k * e_hat)
    bh_s = stack(bet * e_hat)
    v_s = stack(v)

    m = _dot_nt(jnp.concatenate([kap_s, r_s], axis=0), jnp.concatenate([b_s, k_s], axis=0), HI)
    c2 = 2 * c
    a = jnp.where(strict, m[0:c2, 0:c2], 0.0)
    bm = jnp.where(strict, m[0:c2, c2:2 * c2], 0.0)
    mb = jnp.where(incl, m[c2:2 * c2, 0:c2], 0.0)
    mk = jnp.where(incl, m[c2:2 * c2, c2:2 * c2], 0.0)

    t_inv = eye2 - a
    pw = a
    steps = int(np.log2(c)) - 1
    for _ in range(steps):
        pw = _dot(pw, pw, HI)
        t_inv = t_inv + _dot(t_inv, pw, HI)

    bv = _dot(bm, v_s, HI)
    x = _dot(t_inv, jnp.concatenate([kap_s, bv], axis=1), HI)
    z = _dot(mb, x, HI)
    rp = r_s - z[:, 0:LANES]
    y0 = _dot(mk, v_s, HI) - z[:, LANES:2 * LANES]
    bhx = _dot_tn(bh_s, x, HI)
    g = jnp.where(eye_l, jnp.exp(ltot), 0.0) - bhx[:, 0:LANES]
    hadd = _dot_tn(kh_s, v_s, HI) - bhx[:, LANES:2 * LANES]

    ys = _dot(rp, h0, HI) + y0
    hn = _dot(g, h0, HI) + hadd
    return ys[0:c, :] + ys[c:c2, :], hn


def _rwkv_scan_kernel(rf, vf, kkf, lwf, kdf, bdf, rb, vb, kkb, lwb, kdb, bdb, yf_ref, yb_ref, h_ref):
    i = pl.program_id(1)

    @pl.when(i == 0)
    def _():
        h_ref[...] = jnp.zeros_like(h_ref)

    c = rf.shape[0]
    c2 = 2 * c
    row = lax.broadcasted_iota(jnp.int32, (c, c), 0)
    col = lax.broadcasted_iota(jnp.int32, (c, c), 1)
    row2 = lax.broadcasted_iota(jnp.int32, (c2, c2), 0)
    col2 = lax.broadcasted_iota(jnp.int32, (c2, c2), 1)
    same = (row2 < c) == (col2 < c)
    rt = jnp.where(row2 < c, row2, row2 - c)
    ct = jnp.where(col2 < c, col2, col2 - c)
    srow = lax.broadcasted_iota(jnp.int32, (c2, LANES), 0)
    slane = lax.broadcasted_iota(jnp.int32, (c2, LANES), 1)
    head_rows = (srow < c) == (slane < HEAD_DIM)
    eye2 = jnp.where(row2 == col2, 1.0, 0.0)
    lrow = lax.broadcasted_iota(jnp.int32, (LANES, LANES), 0)
    lcol = lax.broadcasted_iota(jnp.int32, (LANES, LANES), 1)
    eye_l = lrow == lcol

    for d, (refs, y_ref) in enumerate((((rf, kdf, vf, kkf, bdf, lwf), yf_ref), ((rb, kdb, vb, kkb, bdb, lwb), yb_ref))):
        reverse = d == 1
        if reverse:
            tri = jnp.where(col >= row, 1.0, 0.0)
            strict = jnp.logical_and(same, ct > rt)
            incl = jnp.logical_and(same, ct >= rt)
        else:
            tri = jnp.where(col <= row, 1.0, 0.0)
            strict = jnp.logical_and(same, ct < rt)
            incl = jnp.logical_and(same, ct <= rt)
        consts = (tri, strict, incl, head_rows, eye2, eye_l)
        for p in range(RWKV_W // LANES):
            sl = slice(p * LANES, (p + 1) * LANES)
            r, k, v, kap, bet, lw = (ref[:, sl] for ref in refs)
            y, hn = _chunk_step(r, k, v, kap, bet, lw, h_ref[d, p], reverse, consts)
            y_ref[:, sl] = y
            h_ref[d, p] = hn


def _rwkv_scan(prep, n_ctx_chunks):
    r, v, kk, lw0, kd0, bd0, lw1, kd1, bd1, _ = prep
    b, t, w = r.shape
    nc = t // SCAN_CHUNK
    ncc = n_ctx_chunks

    def bwd_chunk(i):
        return jnp.where(i < ncc, ncc - 1 - i, nc - 1 + ncc - i)

    fspec = pl.BlockSpec((None, SCAN_CHUNK, w), lambda bi, i: (bi, i, 0))
    bspec = pl.BlockSpec((None, SCAN_CHUNK, w), lambda bi, i: (bi, bwd_chunk(i), 0))
    return pl.pallas_call(
        _rwkv_scan_kernel,
        grid=(b, nc),
        in_specs=[fspec] * 6 + [bspec] * 6,
        out_specs=[fspec, bspec],
        out_shape=[jax.ShapeDtypeStruct((b, t, w), F32)] * 2,
        scratch_shapes=[pltpu.VMEM((2, w // LANES, LANES, LANES), F32)],
        compiler_params=_cparams("parallel", "arbitrary"),
        name="rwkv_scan",
    )(r, v, kk, lw0, kd0, bd0, r, v, kk, lw1, kd1, bd1)


def _qk_prep_kernel(aq_ref, ak_ref, cos_ref, sa_ref, sb_ref, qn_ref, kn_ref, ones_ref, q_o, k_o):
    cos, sin_a, sin_b = cos_ref[...], sa_ref[...], sb_ref[...]
    ones = ones_ref[...]
    lane = lax.broadcasted_iota(jnp.int32, cos.shape, 1)

    def norm_rope(x, g):
        ms = _dot(x * x, ones, HI) * (1.0 / HEAD_DIM)
        xn = x * lax.rsqrt(ms + RMS_EPS) * g
        return (xn * cos + pltpu.roll(xn, LANES - HEAD_DIM // 4, 1) * sin_a
                + pltpu.roll(xn, HEAD_DIM // 4, 1) * sin_b)

    for j in range(GQA_W // LANES):
        q = norm_rope(aq_ref[:, j * LANES:(j + 1) * LANES], qn_ref[...]) * (HEAD_DIM ** -0.5)
        q_o[:, (2 * j) * LANES:(2 * j + 1) * LANES] = jnp.where(lane < HEAD_DIM, q, 0.0).astype(BF16)
        q_o[:, (2 * j + 1) * LANES:(2 * j + 2) * LANES] = jnp.where(lane >= HEAD_DIM, q, 0.0).astype(BF16)
    k_o[...] = norm_rope(ak_ref[...], kn_ref[...]).astype(BF16)


def _qk_prep(aq, ak, cos, sin_a, sin_b, qn, kn, ones_pair):
    b, t, _ = aq.shape
    nt = t // TOKEN_TILE
    tok = lambda width: pl.BlockSpec((None, TOKEN_TILE, width), lambda bi, i: (bi, i, 0))
    tab = pl.BlockSpec((TOKEN_TILE, LANES), lambda bi, i: (i, 0))
    full = lambda shape: pl.BlockSpec(shape, lambda bi, i: (0,) * len(shape))
    return pl.pallas_call(
        _qk_prep_kernel,
        grid=(b, nt),
        in_specs=[tok(GQA_W), tok(GQA_KV_W), tab, tab, tab, full((1, LANES)), full((1, LANES)), full((LANES, LANES))],
        out_specs=[tok(2 * GQA_W), tok(GQA_KV_W)],
        out_shape=[jax.ShapeDtypeStruct((b, t, 2 * GQA_W), BF16), jax.ShapeDtypeStruct((b, t, GQA_KV_W), BF16)],
        compiler_params=_cparams("parallel", "parallel"),
        name="qk_prep",
    )(aq, ak, cos, sin_a, sin_b, qn, kn, ones_pair)


def _gqa_kernel(q_ref, k_ref, v_ref, o_ref, *, n_ctx_chunks, n_chunks):
    i = pl.program_id(1)
    tm = q_ref.shape[0]
    n_kv = jnp.where(i == 0, n_ctx_chunks, n_chunks)
    lane = lax.broadcasted_iota(jnp.int32, (tm, LANES), 1)
    for j in range(GQA_W // LANES):
        halves = []
        for half in range(2):
            qh = q_ref[:, (2 * j + half) * LANES:(2 * j + half + 1) * LANES]

            def body(c, carry, qh=qh):
                m, l, acc = carry
                off = pl.multiple_of(c * KV_CHUNK, KV_CHUNK)
                kc = k_ref[pl.ds(off, KV_CHUNK), :]
                vc = v_ref[pl.ds(off, KV_CHUNK), :]
                s = _dot_nt(qh, kc)
                m_new = jnp.maximum(m, jnp.max(s, axis=-1, keepdims=True))
                alpha = jnp.exp(m - m_new)
                p = jnp.exp(s - m_new)
                l = alpha * l + jnp.sum(p, axis=-1, keepdims=True)
                acc = alpha * acc + _dot(p.astype(BF16), vc)
                return m_new, l, acc

            init = (jnp.full((tm, 1), -jnp.inf, F32), jnp.zeros((tm, 1), F32), jnp.zeros((tm, LANES), F32))
            _, l, acc = lax.fori_loop(0, n_kv, body, init)
            halves.append(acc / l)
        o_ref[:, j * LANES:(j + 1) * LANES] = jnp.where(lane < HEAD_DIM, halves[0], halves[1])


def _gqa(q6, k, v, n_ctx):
    b, t, _ = q6.shape
    nt = t // TOKEN_TILE
    assert n_ctx == TOKEN_TILE, "the context must fill exactly the first query tile"
    kv = pl.BlockSpec((None, t, GQA_KV_W), lambda bi, i: (bi, 0, 0))
    return pl.pallas_call(
        functools.partial(_gqa_kernel, n_ctx_chunks=n_ctx // KV_CHUNK, n_chunks=t // KV_CHUNK),
        grid=(b, nt),
        in_specs=[pl.BlockSpec((None, TOKEN_TILE, 2 * GQA_W), lambda bi, i: (bi, i, 0)), kv, kv],
        out_specs=pl.BlockSpec((None, TOKEN_TILE, GQA_W), lambda bi, i: (bi, i, 0)),
        out_shape=jax.ShapeDtypeStruct((b, t, GQA_W), F32),
        compiler_params=_cparams("parallel", "parallel"),
        name="gqa",
    )(q6, k, v)


def _natten_kernel(q_ref, k_ref, v_ref, bias_ref, o_ref, *, n_ctx, rows):
    i = pl.program_id(1)
    nq = q_ref.shape[0]
    ncb = n_ctx // nq
    win = NA_WIN_ROWS * GRID_W
    r = i - ncb
    r_start = jnp.clip(r - NA_WIN_ROWS // 2, 0, rows - NA_WIN_ROWS)
    koff = pl.multiple_of(n_ctx + r_start * GRID_W, GRID_W)
    lane = lax.broadcasted_iota(jnp.int32, (nq, LANES), 1)
    for pair in range(NA_W // LANES):
        sl = slice(pair * LANES, (pair + 1) * LANES)
        kw = k_ref[pl.ds(koff, win), sl]
        vw = v_ref[pl.ds(koff, win), sl]
        kc = k_ref[0:n_ctx, sl]
        vc = v_ref[0:n_ctx, sl]
        halves = []
        for half in range(2):
            hd = 2 * pair + half
            qh = q_ref[:, hd * LANES:(hd + 1) * LANES]
            s_w = _dot_nt(qh, kw) + bias_ref[hd]
            s_c = _dot_nt(qh, kc)
            m = jnp.maximum(jnp.max(s_w, axis=-1, keepdims=True), jnp.max(s_c, axis=-1, keepdims=True))
            p_w = jnp.exp(s_w - m)
            p_c = jnp.exp(s_c - m)
            l = jnp.sum(p_w, axis=-1, keepdims=True) + jnp.sum(p_c, axis=-1, keepdims=True)
            halves.append((_dot(p_w.astype(BF16), vw) + _dot(p_c.astype(BF16), vc)) / l)
        o_ref[:, sl] = jnp.where(lane < HEAD_DIM, halves[0], halves[1])


def _natten(nq4, nk, nv, bias, n_ctx):
    b, t, _ = nk.shape
    rows = (t - n_ctx) // GRID_W
    assert rows >= NA_WIN_ROWS and n_ctx % GRID_W == 0
    ncb = n_ctx // GRID_W
    win = NA_WIN_ROWS * GRID_W

    def bias_row(i):
        r = i - ncb
        r_start = jnp.clip(r - NA_WIN_ROWS // 2, 0, rows - NA_WIN_ROWS)
        return jnp.where(i < ncb, NA_WIN_ROWS, r_start - r + NA_WIN_ROWS - 1)

    kv = pl.BlockSpec((None, t, NA_W), lambda bi, i: (bi, 0, 0))
    return pl.pallas_call(
        functools.partial(_natten_kernel, n_ctx=n_ctx, rows=rows),
        grid=(b, t // GRID_W),
        in_specs=[
            pl.BlockSpec((None, GRID_W, NA_HEADS * LANES), lambda bi, i: (bi, i, 0)),
            kv, kv,
            pl.BlockSpec((NA_HEADS, None, GRID_W, win), lambda bi, i: (0, bias_row(i), 0, 0)),
        ],
        out_specs=pl.BlockSpec((None, GRID_W, NA_W), lambda bi, i: (bi, i, 0)),
        out_shape=jax.ShapeDtypeStruct((b, t, NA_W), F32),
        compiler_params=_cparams("parallel", "parallel"),
        name="natten",
    )(nq4, nk, nv, bias)


def _natten_bias(rpb):
    cols = np.arange(GRID_W)
    c_start = np.clip(cols - NA_WIN_COLS // 2, 0, GRID_W - NA_WIN_COLS)
    key = np.arange(GRID_W)
    valid = (key[None, :] >= c_start[:, None]) & (key[None, :] < c_start[:, None] + NA_WIN_COLS)
    dc = np.clip(key[None, :] - cols[:, None] + NA_WIN_COLS - 1, 0, 2 * NA_WIN_COLS - 2)
    tabs = []
    for o in range(NA_WIN_ROWS):
        g = rpb[:, o:o + NA_WIN_ROWS, :][:, :, dc]
        g = jnp.where(valid[None, None], g, MASK_BIAS)
        tabs.append(jnp.transpose(g, (0, 2, 1, 3)).reshape(NA_HEADS, GRID_W, NA_WIN_ROWS * GRID_W))
    tabs.append(jnp.full_like(tabs[0], MASK_BIAS))
    return jnp.stack(tabs, axis=1).astype(F32)


def _out_kernel(x_ref, yf_ref, yb_ref, bonus_ref, ya_ref, yn_ref, g_ref, mod_ref, w_ref,
                gng_ref, gnb_ref, lng_ref, lnb_ref, ones_ref, o_ref, *, alpha):
    ones = ones_ref[...]
    ys = yf_ref[...] + yb_ref[...]
    mu = _dot(ys, ones, HI) * (1.0 / HEAD_DIM)
    dv = ys - mu
    var = _dot(dv * dv, ones, HI) * (1.0 / HEAD_DIM)
    yr = dv * lax.rsqrt(var + GN_EPS) * gng_ref[...] + gnb_ref[...] + bonus_ref[...]
    g = g_ref[...].astype(F32)
    parts = (
        yr * _silu(g[:, 0:RWKV_W]),
        ya_ref[...] * _silu(g[:, RWKV_W:RWKV_W + GQA_W]),
        yn_ref[...] * _silu(g[:, RWKV_W + GQA_W:]),
    )
    mixed = jnp.concatenate([p.astype(BF16) for p in parts], axis=1)
    y = _dot(mixed, w_ref[...])
    xn = alpha * x_ref[...] + mod_ref[2:3, :] * y
    mean = jnp.mean(xn, axis=-1, keepdims=True)
    xc = xn - mean
    var = jnp.mean(xc * xc, axis=-1, keepdims=True)
    o_ref[...] = xc * lax.rsqrt(var + LN_EPS) * lng_ref[...] + lnb_ref[...]


def _output(xc, yf, yb, bonus, ya, yn, gates, modsel, w_out, gn_g, gn_b, ln_g, ln_b, ones_heads, alpha, n_ctx_tiles,
            skip_tiles):
    b, t, d = xc.shape
    nt = t // TOKEN_TILE - skip_tiles
    tok = lambda width: pl.BlockSpec((None, TOKEN_TILE, width), lambda bi, i: (bi, i + skip_tiles, 0))
    full = lambda shape: pl.BlockSpec(shape, lambda bi, i: (0,) * len(shape))
    return pl.pallas_call(
        functools.partial(_out_kernel, alpha=alpha),
        grid=(b, nt),
        in_specs=[
            tok(d), tok(RWKV_W), tok(RWKV_W), tok(RWKV_W), tok(GQA_W), tok(NA_W), tok(RWKV_W + GQA_W + NA_W),
            pl.BlockSpec((None, None, 3, d), lambda bi, i: (bi, jnp.where(i + skip_tiles < n_ctx_tiles, 0, 1), 0, 0)),
            full(w_out.shape), full((1, RWKV_W)), full((1, RWKV_W)), full((1, d)), full((1, d)), full((RWKV_W, RWKV_W)),
        ],
        out_specs=pl.BlockSpec((None, TOKEN_TILE, d), lambda bi, i: (bi, i, 0)),
        out_shape=jax.ShapeDtypeStruct((b, nt * TOKEN_TILE, d), F32),
        compiler_params=_cparams("parallel", "parallel"),
        name="out",
    )(xc, yf, yb, bonus, ya, yn, gates, modsel, w_out, gn_g, gn_b, ln_g, ln_b, ones_heads)


def _rope_tables(n_ctx, n_lat):
    t = jnp.arange(n_lat, dtype=jnp.int32)
    row = (t // GRID_W).astype(F32)
    col = (t % GRID_W).astype(F32)
    n_freq = HEAD_DIM // 4
    inv_freq = ROPE_THETA ** (-jnp.arange(n_freq, dtype=F32) / n_freq)
    ang_r = row[:, None] * inv_freq
    ang_c = col[:, None] * inv_freq
    ang = jnp.concatenate([ang_r, ang_r, ang_c, ang_c], axis=-1)
    cos = jnp.concatenate([jnp.ones((n_ctx, HEAD_DIM), F32), jnp.cos(ang)], axis=0)
    sin = jnp.concatenate([jnp.zeros((n_ctx, HEAD_DIM), F32), jnp.sin(ang)], axis=0)
    first = (np.arange(HEAD_DIM) % (HEAD_DIM // 2)) < HEAD_DIM // 4
    sin_a = jnp.where(first, -sin, 0.0)
    sin_b = jnp.where(first, 0.0, sin)
    pair = lambda z: jnp.concatenate([z, z], axis=-1)
    return pair(cos), pair(sin_a), pair(sin_b)


def _block_ones(n_heads):
    blk = np.kron(np.eye(n_heads, dtype=np.float32), np.ones((HEAD_DIM, HEAD_DIM), np.float32))
    return jnp.asarray(blk)


def kernel(x, c, ctx, c_ctx, w_mod, b_mod, w_in, w_out, rwkv_conv, decay_w0, decay_w2, iclr_a0, iclr_a2, rwkv_k_k, rwkv_k_a, rwkv_r_k, rwkv_gn_g, rwkv_gn_b, gqa_q_norm, gqa_k_norm, na_rpb, ln_g, ln_b):
    b, n_lat, d = x.shape
    n_ctx = ctx.shape[1]
    depth = w_mod.shape[0]
    alpha = float((2 * depth) ** 0.25)
    n_ctx_tiles = n_ctx // TOKEN_TILE
    assert n_ctx % TOKEN_TILE == 0 and n_lat % TOKEN_TILE == 0 and n_lat % GRID_W == 0

    n_rows = -(-(b + 1) // SUBLANES) * SUBLANES
    cvec = jnp.zeros((n_rows, d), F32).at[:b].set(c).at[b].set(c_ctx)
    mod = _modulation(cvec, w_mod, b_mod).reshape(depth, n_rows, 3, d)
    mod_ctx = jnp.broadcast_to(mod[:, b][:, None], (depth, b, 3, d))
    modsel = jnp.stack([mod_ctx, mod[:, :b]], axis=2)

    sizes = (RWKV_CONV_W, RWKV_W, GQA_W, GQA_KV_W, GQA_KV_W, GQA_W, NA_W, NA_W, NA_W, NA_W)
    offs = np.concatenate([[0], np.cumsum(sizes)])
    seg = lambda n: np.arange(offs[n], offs[n + 1])
    head_perm = np.concatenate([np.arange(h * HEAD_DIM, (h + 1) * HEAD_DIM) for h in (0, 3, 1, 4, 2, 5)])
    cols = np.concatenate([seg(0), seg(2)[head_perm], seg(3), seg(4), seg(6), seg(7), seg(8),
                           seg(1), seg(5)[head_perm], seg(9)])
    w_in_p = w_in[:, :, cols].astype(BF16)
    out_rows = np.concatenate([np.arange(RWKV_W), RWKV_W + head_perm, np.arange(RWKV_W + GQA_W, RWKV_W + GQA_W + NA_W)])
    w_out_p = w_out[:, out_rows, :].astype(BF16)

    w = RWKV_W
    wlr = jnp.zeros((depth, LANES, 4 * w), F32)
    for dd in range(2):
        wlr = wlr.at[:, dd * LOW_RANK:(dd + 1) * LOW_RANK, dd * w:(dd + 1) * w].set(decay_w2[:, dd])
        wlr = wlr.at[:, (2 + dd) * LOW_RANK:(3 + dd) * LOW_RANK, (2 + dd) * w:(3 + dd) * w].set(iclr_a2[:, dd])
    b0 = jnp.concatenate([decay_w0[:, 0], decay_w0[:, 1], iclr_a0[:, 0], iclr_a0[:, 1]], axis=-1)[:, None, :]

    cos, sin_a, sin_b = _rope_tables(n_ctx, n_lat)
    ones_heads = _block_ones(RWKV_HEADS)
    ones_pair = _block_ones(2)
    pair = lambda z: jnp.concatenate([z, z], axis=-1)[:, None, :]
    qn, kn = pair(gqa_q_norm), pair(gqa_k_norm)

    xc = jnp.concatenate([ctx, x], axis=1)
    for l in range(depth):
        ru, aq, ak, av, nq4, nk, nv, gates = _projection(xc, modsel[l], w_in_p[l], n_ctx_tiles)
        prep = _rwkv_prep(ru, rwkv_conv[l], wlr[l], b0[l], rwkv_k_k[l][None], rwkv_k_a[l][None],
                          rwkv_r_k[l].reshape(1, w), ones_heads, n_ctx_tiles)
        yf, yb = _rwkv_scan(prep, n_ctx // SCAN_CHUNK)
        q6, kr = _qk_prep(aq, ak, cos, sin_a, sin_b, qn[l], kn[l], ones_pair)
        ya = _gqa(q6, kr, av, n_ctx)
        yn = _natten(nq4, nk, nv, _natten_bias(na_rpb[l]), n_ctx)
        last = l == depth - 1
        xc = _output(xc, yf, yb, prep[9], ya, yn, gates, modsel[l], w_out_p[l], rwkv_gn_g[l][None], rwkv_gn_b[l][None],
                     ln_g[l][None], ln_b[l][None], ones_heads, alpha, n_ctx_tiles, n_ctx_tiles if last else 0)
    return xc
```

```python
import functools

import numpy as np
import jax
import jax.numpy as jnp
from jax import lax
from jax.experimental import pallas as pl
from jax.experimental.pallas import tpu as pltpu

F32 = jnp.float32
BF16 = jnp.bfloat16
HI = lax.Precision.HIGHEST

HEAD_DIM = 64
RWKV_HEADS = 6
GQA_Q_HEADS = 6
GQA_KV_HEADS = 2
NA_HEADS = 4
RWKV_W = RWKV_HEADS * HEAD_DIM
GQA_W = GQA_Q_HEADS * HEAD_DIM
GQA_KV_W = GQA_KV_HEADS * HEAD_DIM
NA_W = NA_HEADS * HEAD_DIM
LOW_RANK = 32
RWKV_CONV_W = 3 * RWKV_W + 4 * LOW_RANK
GRID_W = 64
NA_WIN_ROWS = 8
NA_WIN_COLS = 16
ROPE_THETA = 10000.0
LN_EPS = 1e-5
RMS_EPS = 1e-6
GN_EPS = 64e-5
KK_EPS = 1e-12
MASK_BIAS = -1e30

LANES = 128
SUBLANES = 8
TOKEN_TILE = 256
SCAN_CHUNK = 64
KV_CHUNKS = (768, 512, 256)
LOG2_E = 1.4426950408889634
VMEM_LIMIT = 56 * 1024 * 1024


def _cparams(*sem):
    return pltpu.CompilerParams(dimension_semantics=sem, vmem_limit_bytes=VMEM_LIMIT)


def _dot(a, b, precision=None):
    return jnp.dot(a, b, preferred_element_type=F32, precision=precision)


def _dot_nt(a, b, precision=None):
    return lax.dot_general(a, b, (((1,), (1,)), ((), ())), preferred_element_type=F32, precision=precision)


def _dot_tn(a, b, precision=None):
    return lax.dot_general(a, b, (((0,), (0,)), ((), ())), preferred_element_type=F32, precision=precision)


def _sigmoid(x):
    return 1.0 / (1.0 + jnp.exp(-x))


def _silu(x):
    return x * _sigmoid(x)


def _mod_kernel(c_ref, w_ref, b_ref, o_ref):
    o_ref[...] = _dot(_silu(c_ref[...]), w_ref[...], HI) + b_ref[...]


def _modulation(cvec, w_mod, b_mod):
    depth, d, d3 = w_mod.shape
    r = cvec.shape[0]
    nj = d3 // d
    return pl.pallas_call(
        _mod_kernel,
        grid=(depth, nj),
        in_specs=[
            pl.BlockSpec((r, d), lambda l, j: (0, 0)),
            pl.BlockSpec((None, d, d), lambda l, j: (l, 0, j)),
            pl.BlockSpec((None, 1, d), lambda l, j: (l, 0, j)),
        ],
        out_specs=pl.BlockSpec((None, r, d), lambda l, j: (l, 0, j)),
        out_shape=jax.ShapeDtypeStruct((depth, r, d3), F32),
        compiler_params=_cparams("parallel", "parallel"),
        name="modulation",
    )(cvec, w_mod, b_mod.reshape(depth, 1, d3))


_PROJ_SLABS = (
    ("ru", RWKV_CONV_W, F32),
    ("aq", GQA_W, F32),
    ("ak", GQA_KV_W, F32),
    ("av", GQA_KV_W, BF16),
    ("nq", NA_W, None),
    ("nk", NA_W, BF16),
    ("nv", NA_W, BF16),
    ("gates", RWKV_W + GQA_W + NA_W, BF16),
)


def _proj_kernel(x_ref, mod_ref, w_ref, ru_ref, aq_ref, ak_ref, av_ref, nq_ref, nk_ref, nv_ref, g_ref):
    shift = mod_ref[0:1, :]
    scale = mod_ref[1:2, :]
    h = (x_ref[...] * (1.0 + scale) + shift).astype(BF16)
    outs = {"ru": ru_ref, "aq": aq_ref, "ak": ak_ref, "av": av_ref, "nk": nk_ref, "nv": nv_ref, "gates": g_ref}
    off = 0
    for name, width, _ in _PROJ_SLABS:
        u = _dot(h, w_ref[:, off:off + width])
        if name == "nq":
            u = u * (HEAD_DIM ** -0.5)
            lane = lax.broadcasted_iota(jnp.int32, (u.shape[0], LANES), 1)
            for hd in range(NA_HEADS):
                pair = u[:, (hd // 2) * LANES:(hd // 2 + 1) * LANES]
                keep = (lane < HEAD_DIM) if hd % 2 == 0 else (lane >= HEAD_DIM)
                nq_ref[:, hd * LANES:(hd + 1) * LANES] = jnp.where(keep, pair, 0.0).astype(BF16)
        else:
            outs[name][...] = u.astype(outs[name].dtype)
        off += width


def _projection(xc, modsel, w_perm, n_ctx_tiles):
    b, t, d = xc.shape
    nt = t // TOKEN_TILE
    n_in = w_perm.shape[1]
    shapes, specs = [], []
    for name, width, dt in _PROJ_SLABS:
        if name == "nq":
            width, dt = NA_HEADS * LANES, BF16
        shapes.append(jax.ShapeDtypeStruct((b, t, width), dt))
        specs.append(pl.BlockSpec((None, TOKEN_TILE, width), lambda bi, i: (bi, i, 0)))
    return pl.pallas_call(
        _proj_kernel,
        grid=(b, nt),
        in_specs=[
            pl.BlockSpec((None, TOKEN_TILE, d), lambda bi, i: (bi, i, 0)),
            pl.BlockSpec((None, None, 3, d), lambda bi, i: (bi, jnp.where(i < n_ctx_tiles, 0, 1), 0, 0)),
            pl.BlockSpec((d, n_in), lambda bi, i: (0, 0)),
        ],
        out_specs=specs,
        out_shape=shapes,
        compiler_params=_cparams("parallel", "parallel"),
        name="proj",
    )(xc, modsel, w_perm)


def _rwkv_prep_kernel(x_ref, xp_ref, xn_ref, cw_ref, wlr_ref, b0_ref, kk_ref, ka_ref, rk_ref, ones_ref,
                      r_o, v_o, kk_o, lw0_o, kd0_o, bd0_o, lw1_o, kd1_o, bd1_o, bonus_o, pad_ref, *, n_ctx_tiles):
    i = pl.program_id(1)
    nt = pl.num_programs(1)
    tm = x_ref.shape[0]
    prev_ok = jnp.logical_and(i != 0, i != n_ctx_tiles)
    next_ok = jnp.logical_and(i != n_ctx_tiles - 1, i != nt - 1)
    pad_ref[0:SUBLANES, :] = jnp.where(prev_ok, xp_ref[...], 0.0)
    pad_ref[SUBLANES:SUBLANES + tm, :] = x_ref[...]
    pad_ref[SUBLANES + tm:2 * SUBLANES + tm, :] = jnp.where(next_ok, xn_ref[...], 0.0)
    x_prev = pad_ref[pl.ds(SUBLANES - 1, tm), :]
    x_next = pad_ref[pl.ds(SUBLANES + 1, tm), :]
    u = cw_ref[0:1, :] * x_prev + cw_ref[1:2, :] * x_ref[...] + cw_ref[2:3, :] * x_next

    w = RWKV_W
    r, k, v = u[:, 0:w], u[:, w:2 * w], u[:, 2 * w:3 * w]
    lr = u[:, 3 * w:3 * w + LANES]
    lane = lax.broadcasted_iota(jnp.int32, lr.shape, 1)
    z = jnp.where(lane < 2 * LOW_RANK, jnp.tanh(lr), lr)
    pre = _dot(z, wlr_ref[...], HI) + b0_ref[...]

    ones = ones_ref[...]
    kk0 = k * kk_ref[...]
    ss = _dot(kk0 * kk0, ones, HI)
    kk = kk0 * lax.rsqrt(jnp.maximum(ss, KK_EPS))
    r_o[...] = r
    v_o[...] = v
    kk_o[...] = kk
    kd_sum = None
    for d, (lw_o, kd_o, bd_o) in enumerate(((lw0_o, kd0_o, bd0_o), (lw1_o, kd1_o, bd1_o))):
        xd = -pre[:, d * w:(d + 1) * w]
        softplus = jnp.maximum(xd, 0.0) + jnp.log(1.0 + jnp.exp(-jnp.abs(xd)))
        lw_o[...] = -jnp.exp(-softplus - 0.5)
        iclr = _sigmoid(pre[:, (2 + d) * w:(3 + d) * w])
        kd = k * (1.0 + (iclr - 1.0) * ka_ref[...])
        kd_o[...] = kd
        bd_o[...] = kk * iclr
        kd_sum = kd if kd_sum is None else kd_sum + kd
    bonus_o[...] = _dot(r * kd_sum * rk_ref[...], ones, HI) * v


def _rwkv_prep(ru, conv_w, wlr, b0, k_k, k_a, r_k, ones_heads, n_ctx_tiles):
    b, t, cw = ru.shape
    nt = t // TOKEN_TILE
    hb = TOKEN_TILE // SUBLANES
    w = RWKV_W
    full = lambda shape: pl.BlockSpec(shape, lambda bi, i: (0,) * len(shape))
    out_spec = pl.BlockSpec((None, TOKEN_TILE, w), lambda bi, i: (bi, i, 0))
    return pl.pallas_call(
        functools.partial(_rwkv_prep_kernel, n_ctx_tiles=n_ctx_tiles),
        grid=(b, nt),
        in_specs=[
            pl.BlockSpec((None, TOKEN_TILE, cw), lambda bi, i: (bi, i, 0)),
            pl.BlockSpec((None, SUBLANES, cw), lambda bi, i: (bi, jnp.maximum(i * hb - 1, 0), 0)),
            pl.BlockSpec((None, SUBLANES, cw), lambda bi, i: (bi, jnp.minimum((i + 1) * hb, t // SUBLANES - 1), 0)),
            full((3, cw)), full((LANES, 4 * w)), full((1, 4 * w)),
            full((1, w)), full((1, w)), full((1, w)), full((w, w)),
        ],
        out_specs=[out_spec] * 10,
        out_shape=[jax.ShapeDtypeStruct((b, t, w), F32)] * 10,
        scratch_shapes=[pltpu.VMEM((TOKEN_TILE + 2 * SUBLANES, cw), F32)],
        compiler_params=_cparams("parallel", "parallel"),
        name="rwkv_prep",
    )(ru, ru, ru, conv_w, wlr, b0, k_k, k_a, r_k, ones_heads)


def _split2(x):
    hi = x.astype(BF16)
    return hi, (x - hi.astype(F32)).astype(BF16)


def _lhs3(x, axis):
    hi, lo = _split2(x)
    return jnp.concatenate([hi, lo, hi], axis=axis)


def _rhs3(x, axis):
    hi, lo = _split2(x)
    return jnp.concatenate([hi, hi, lo], axis=axis)


def _bdot(a, b):
    return lax.dot_general(a, b, (((2,), (1,)), ((0,), (0,))), preferred_element_type=F32)


def _bdot_nt(a, b):
    return lax.dot_general(a, b, (((2,), (2,)), ((0,), (0,))), preferred_element_type=F32)


def _bdot_tn(a, b):
    return lax.dot_general(a, b, (((1,), (1,)), ((0,), (0,))), preferred_element_type=F32)


def _chunk_step(r, k, v, kap, bet, lw, h0, consts):
    tri, rev, strict, incl, head_rows, eye2, eye_l = consts
    c = r.shape[1]
    l1 = lw.astype(BF16)
    l2 = (lw - l1.astype(F32)).astype(BF16)
    l3 = (lw - l1.astype(F32) - l2.astype(F32)).astype(BF16)
    lcs = _bdot(tri, jnp.concatenate([l1, l2, l3], axis=2))
    lc = lcs[:, :, 0:LANES] + lcs[:, :, LANES:2 * LANES] + lcs[:, :, 2 * LANES:3 * LANES]
    ltot = jnp.where(rev, lc[:, 0:1, :], lc[:, c - 1:c, :])
    e_in = jnp.exp(lc)
    e_inv = jnp.exp(-lc)
    e_hat = jnp.exp(ltot - lc)

    def stack(x):
        return jnp.where(head_rows, jnp.concatenate([x, x], axis=1), 0.0)

    r_s = stack(r * e_in)
    kap_s = stack(kap * jnp.exp(lc - lw))
    k_s = stack(k * e_inv)
    b_s = stack(bet * e_inv)
    kh_s = stack(k * e_hat)
    bh_s = stack(bet * e_hat)
    v_s = stack(v)

    m = _bdot_nt(_lhs3(jnp.concatenate([kap_s, r_s], axis=1), 2), _rhs3(jnp.concatenate([b_s, k_s], axis=1), 2))
    c2 = 2 * c
    a = jnp.where(strict, m[:, 0:c2, 0:c2], 0.0)
    bm = jnp.where(strict, m[:, 0:c2, c2:2 * c2], 0.0)
    mb = jnp.where(incl, m[:, c2:2 * c2, 0:c2], 0.0)
    mk = jnp.where(incl, m[:, c2:2 * c2, c2:2 * c2], 0.0)

    t_inv = eye2 - a
    pw = a
    for _ in range(int(np.log2(c)) - 1):
        pw = _bdot(_lhs3(pw, 2), _rhs3(pw, 1))
        t_inv = t_inv + _bdot(_lhs3(t_inv, 2), _rhs3(pw, 1))

    v_r = _rhs3(v_s, 1)
    bv = _bdot(_lhs3(bm, 2), v_r)
    x = _bdot(_lhs3(t_inv, 2), _rhs3(jnp.concatenate([kap_s, bv], axis=2), 1))
    x_r = _rhs3(x, 1)
    z = _bdot(_lhs3(mb, 2), x_r)
    rp = r_s - z[:, :, 0:LANES]
    y0 = _bdot(_lhs3(mk, 2), v_r) - z[:, :, LANES:2 * LANES]
    bhx = _bdot_tn(_lhs3(bh_s, 1), x_r)
    g = jnp.where(eye_l, jnp.exp(ltot), 0.0) - bhx[:, :, 0:LANES]
    hadd = _bdot_tn(_lhs3(kh_s, 1), v_r) - bhx[:, :, LANES:2 * LANES]

    h_r = _rhs3(h0, 1)
    ys = _bdot(_lhs3(rp, 2), h_r) + y0
    hn = _bdot(_lhs3(g, 2), h_r) + hadd
    return ys[:, 0:c, :] + ys[:, c:c2, :], hn


def _rwkv_scan_kernel(rf, vf, kkf, lwf, kdf, bdf, rb, vb, kkb, lwb, kdb, bdb, yf_ref, yb_ref, h_ref):
    i = pl.program_id(1)

    @pl.when(i == 0)
    def _():
        h_ref[...] = jnp.zeros_like(h_ref)

    c = rf.shape[0]
    c2 = 2 * c
    n_pairs = RWKV_W // LANES
    n = 2 * n_pairs
    iota = lambda shape, dim: lax.broadcasted_iota(jnp.int32, shape, dim)
    rev = iota((n, 1, 1), 0) >= n_pairs
    sign = lambda shape: jnp.where(iota(shape, 0) >= n_pairs, 1, -1)
    tri = jnp.where((iota((n, c, c), 2) - iota((n, c, c), 1)) * sign((n, c, c)) >= 0, 1.0, 0.0).astype(BF16)
    row2, col2 = iota((n, c2, c2), 1), iota((n, c2, c2), 2)
    same = (row2 < c) == (col2 < c)
    rt = jnp.where(row2 < c, row2, row2 - c)
    ct = jnp.where(col2 < c, col2, col2 - c)
    before = (ct - rt) * sign((n, c2, c2))
    strict = jnp.logical_and(same, before > 0)
    incl = jnp.logical_and(same, before >= 0)
    head_rows = (iota((n, c2, LANES), 1) < c) == (iota((n, c2, LANES), 2) < HEAD_DIM)
    eye2 = jnp.where(row2 == col2, 1.0, 0.0)
    eye_l = iota((n, LANES, LANES), 1) == iota((n, LANES, LANES), 2)
    consts = (tri, rev, strict, incl, head_rows, eye2, eye_l)

    def chains(fref, bref):
        return jnp.stack([ref[:, p * LANES:(p + 1) * LANES] for ref in (fref, bref) for p in range(n_pairs)], axis=0)

    y, hn = _chunk_step(chains(rf, rb), chains(kdf, kdb), chains(vf, vb), chains(kkf, kkb), chains(bdf, bdb),
                        chains(lwf, lwb), h_ref[...], consts)
    h_ref[...] = hn
    for p in range(n_pairs):
        yf_ref[:, p * LANES:(p + 1) * LANES] = y[p]
        yb_ref[:, p * LANES:(p + 1) * LANES] = y[n_pairs + p]


def _rwkv_scan(prep, n_ctx_chunks):
    r, v, kk, lw0, kd0, bd0, lw1, kd1, bd1, _ = prep
    b, t, w = r.shape
    nc = t // SCAN_CHUNK
    ncc = n_ctx_chunks

    def bwd_chunk(i):
        return jnp.where(i < ncc, ncc - 1 - i, nc - 1 + ncc - i)

    fspec = pl.BlockSpec((None, SCAN_CHUNK, w), lambda bi, i: (bi, i, 0))
    bspec = pl.BlockSpec((None, SCAN_CHUNK, w), lambda bi, i: (bi, bwd_chunk(i), 0))
    return pl.pallas_call(
        _rwkv_scan_kernel,
        grid=(b, nc),
        in_specs=[fspec] * 6 + [bspec] * 6,
        out_specs=[fspec, bspec],
        out_shape=[jax.ShapeDtypeStruct((b, t, w), F32)] * 2,
        scratch_shapes=[pltpu.VMEM((2 * (w // LANES), LANES, LANES), F32)],
        compiler_params=_cparams("parallel", "arbitrary"),
        name="rwkv_scan",
    )(r, v, kk, lw0, kd0, bd0, r, v, kk, lw1, kd1, bd1)


def _qk_prep_kernel(aq_ref, ak_ref, av_ref, cos_ref, sa_ref, sb_ref, qn_ref, kn_ref, ones_ref, qt_o, k_o, vt_o):
    cos, sin_a, sin_b = cos_ref[...], sa_ref[...], sb_ref[...]
    ones = ones_ref[...]
    tm = cos.shape[0]
    row = lax.broadcasted_iota(jnp.int32, (LANES, tm), 0)

    def norm_rope(x, g):
        ms = _dot(x * x, ones, HI) * (1.0 / HEAD_DIM)
        xn = x * lax.rsqrt(ms + RMS_EPS) * g
        return (xn * cos + pltpu.roll(xn, LANES - HEAD_DIM // 4, 1) * sin_a
                + pltpu.roll(xn, HEAD_DIM // 4, 1) * sin_b)

    for j in range(GQA_W // LANES):
        q = norm_rope(aq_ref[:, j * LANES:(j + 1) * LANES], qn_ref[...]) * (HEAD_DIM ** -0.5 * LOG2_E)
        qt = q.T
        qt_o[(2 * j) * LANES:(2 * j + 1) * LANES, :] = jnp.where(row < HEAD_DIM, qt, 0.0).astype(BF16)
        qt_o[(2 * j + 1) * LANES:(2 * j + 2) * LANES, :] = jnp.where(row >= HEAD_DIM, qt, 0.0).astype(BF16)
    k_o[...] = norm_rope(ak_ref[...], kn_ref[...]).astype(BF16)
    vt = av_ref[...].astype(F32).T
    one = jnp.ones((HEAD_DIM, tm), F32)
    vt_o[...] = jnp.concatenate([vt[0:HEAD_DIM], one, vt[HEAD_DIM:2 * HEAD_DIM], one], axis=0).astype(BF16)


def _qk_prep(aq, ak, av, cos, sin_a, sin_b, qn, kn, ones_pair, kv_chunk):
    b, t, _ = aq.shape
    nt = t // TOKEN_TILE
    per = kv_chunk // TOKEN_TILE
    tok = lambda width: pl.BlockSpec((None, TOKEN_TILE, width), lambda bi, i: (bi, i, 0))
    tab = pl.BlockSpec((TOKEN_TILE, LANES), lambda bi, i: (i, 0))
    full = lambda shape: pl.BlockSpec(shape, lambda bi, i: (0,) * len(shape))
    return pl.pallas_call(
        _qk_prep_kernel,
        grid=(b, nt),
        in_specs=[tok(GQA_W), tok(GQA_KV_W), tok(GQA_KV_W), tab, tab, tab,
                  full((1, LANES)), full((1, LANES)), full((LANES, LANES))],
        out_specs=[
            pl.BlockSpec((None, 2 * GQA_W, TOKEN_TILE), lambda bi, i: (bi, 0, i)),
            tok(GQA_KV_W),
            pl.BlockSpec((None, None, 2 * LANES, TOKEN_TILE), lambda bi, i: (bi, i // per, 0, i % per)),
        ],
        out_shape=[
            jax.ShapeDtypeStruct((b, 2 * GQA_W, t), BF16),
            jax.ShapeDtypeStruct((b, t, GQA_KV_W), BF16),
            jax.ShapeDtypeStruct((b, t // kv_chunk, 2 * LANES, kv_chunk), BF16),
        ],
        compiler_params=_cparams("parallel", "parallel"),
        name="qk_prep",
    )(aq, ak, av, cos, sin_a, sin_b, qn, kn, ones_pair)


def _gqa_kernel(qt_ref, k_ref, vt_ref, o_ref, acc_ref, first_ref):
    n_chunks = k_ref.shape[0]
    tq = qt_ref.shape[1]
    nj = GQA_W // LANES
    for g in range(GQA_KV_HEADS):
        qt = jnp.concatenate([qt_ref[(2 * j + g) * LANES:(2 * j + g + 1) * LANES, :] for j in range(nj)], axis=1)
        acc_ref[...] = jnp.zeros_like(acc_ref)

        def body(c, m, qt=qt, g=g):
            s = _dot(k_ref[c], qt)
            vt = vt_ref[c, g * LANES:(g + 1) * LANES, :]
            m_new = jnp.maximum(m, jnp.max(s, axis=0, keepdims=True))
            alpha = jnp.exp2(m - m_new)
            p = jnp.exp2(s - m_new).astype(BF16)
            for j in range(nj):
                sl = slice(j * tq, (j + 1) * tq)
                acc_ref[j] = alpha[:, sl] * acc_ref[j] + _dot(vt, p[:, sl])
            return m_new

        lax.fori_loop(0, n_chunks, body, jnp.full((1, nj * tq), -jnp.inf, F32))
        for j in range(nj):
            acc = acc_ref[j]
            out_t = acc[0:HEAD_DIM] / acc[HEAD_DIM:HEAD_DIM + 1]
            if g == 0:
                first_ref[j] = out_t
            else:
                o_ref[:, j * LANES:(j + 1) * LANES] = jnp.concatenate([first_ref[j], out_t], axis=0).T


def _gqa(qt, k4, vt4, q_start, n_q):
    b = qt.shape[0]
    _, n_chunks, kv_chunk, _ = k4.shape
    tq = TOKEN_TILE
    q0 = q_start // tq
    return pl.pallas_call(
        _gqa_kernel,
        grid=(b, n_q // tq),
        in_specs=[
            pl.BlockSpec((None, 2 * GQA_W, tq), lambda bi, i: (bi, 0, i + q0)),
            pl.BlockSpec((None, n_chunks, kv_chunk, GQA_KV_W), lambda bi, i: (bi, 0, 0, 0)),
            pl.BlockSpec((None, n_chunks, 2 * LANES, kv_chunk), lambda bi, i: (bi, 0, 0, 0)),
        ],
        out_specs=pl.BlockSpec((None, tq, GQA_W), lambda bi, i: (bi, i, 0)),
        out_shape=jax.ShapeDtypeStruct((b, n_q, GQA_W), F32),
        scratch_shapes=[pltpu.VMEM((GQA_W // LANES, LANES, tq), F32), pltpu.VMEM((GQA_W // LANES, HEAD_DIM, tq), F32)],
        compiler_params=_cparams("parallel", "parallel"),
        name="gqa",
    )(qt, k4, vt4)


def _natten_kernel(q_ref, k_ref, v_ref, bias_ref, o_ref, *, n_ctx, rows):
    i = pl.program_id(1)
    nq = q_ref.shape[0]
    ncb = n_ctx // nq
    win = NA_WIN_ROWS * GRID_W
    r = i - ncb
    r_start = jnp.clip(r - NA_WIN_ROWS // 2, 0, rows - NA_WIN_ROWS)
    koff = pl.multiple_of(n_ctx + r_start * GRID_W, GRID_W)
    lane = lax.broadcasted_iota(jnp.int32, (nq, LANES), 1)
    for pair in range(NA_W // LANES):
        sl = slice(pair * LANES, (pair + 1) * LANES)
        kw = k_ref[pl.ds(koff, win), sl]
        vw = v_ref[pl.ds(koff, win), sl]
        kc = k_ref[0:n_ctx, sl]
        vc = v_ref[0:n_ctx, sl]
        halves = []
        for half in range(2):
            hd = 2 * pair + half
            qh = q_ref[:, hd * LANES:(hd + 1) * LANES]
            s_w = _dot_nt(qh, kw) + bias_ref[hd]
            s_c = _dot_nt(qh, kc)
            m = jnp.maximum(jnp.max(s_w, axis=-1, keepdims=True), jnp.max(s_c, axis=-1, keepdims=True))
            p_w = jnp.exp(s_w - m)
            p_c = jnp.exp(s_c - m)
            l = jnp.sum(p_w, axis=-1, keepdims=True) + jnp.sum(p_c, axis=-1, keepdims=True)
            halves.append((_dot(p_w.astype(BF16), vw) + _dot(p_c.astype(BF16), vc)) / l)
        o_ref[:, sl] = jnp.where(lane < HEAD_DIM, halves[0], halves[1])


def _natten(nq4, nk, nv, bias, n_ctx):
    b, t, _ = nk.shape
    rows = (t - n_ctx) // GRID_W
    assert rows >= NA_WIN_ROWS and n_ctx % GRID_W == 0
    ncb = n_ctx // GRID_W
    win = NA_WIN_ROWS * GRID_W

    def bias_row(i):
        r = i - ncb
        r_start = jnp.clip(r - NA_WIN_ROWS // 2, 0, rows - NA_WIN_ROWS)
        return jnp.where(i < ncb, NA_WIN_ROWS, r_start - r + NA_WIN_ROWS - 1)

    kv = pl.BlockSpec((None, t, NA_W), lambda bi, i: (bi, 0, 0))
    return pl.pallas_call(
        functools.partial(_natten_kernel, n_ctx=n_ctx, rows=rows),
        grid=(b, t // GRID_W),
        in_specs=[
            pl.BlockSpec((None, GRID_W, NA_HEADS * LANES), lambda bi, i: (bi, i, 0)),
            kv, kv,
            pl.BlockSpec((NA_HEADS, None, GRID_W, win), lambda bi, i: (0, bias_row(i), 0, 0)),
        ],
        out_specs=pl.BlockSpec((None, GRID_W, NA_W), lambda bi, i: (bi, i, 0)),
        out_shape=jax.ShapeDtypeStruct((b, t, NA_W), F32),
        compiler_params=_cparams("parallel", "parallel"),
        name="natten",
    )(nq4, nk, nv, bias)


def _natten_bias(rpb):
    cols = np.arange(GRID_W)
    c_start = np.clip(cols - NA_WIN_COLS // 2, 0, GRID_W - NA_WIN_COLS)
    key = np.arange(GRID_W)
    valid = (key[None, :] >= c_start[:, None]) & (key[None, :] < c_start[:, None] + NA_WIN_COLS)
    dc = key[None, :] - cols[:, None] + NA_WIN_COLS - 1
    place = (valid[:, :, None] & (dc[:, :, None] == np.arange(2 * NA_WIN_COLS - 1))).astype(np.float32)
    mask = np.where(valid, 0.0, MASK_BIAS).astype(np.float32)
    rows = jnp.stack([rpb[:, o:o + NA_WIN_ROWS, :] for o in range(NA_WIN_ROWS)], axis=1)
    dense = jnp.einsum("hopd,cxd->hocpx", rows, jnp.asarray(place), precision=HI) + mask[None, None, :, None, :]
    dense = dense.reshape(NA_HEADS, NA_WIN_ROWS, GRID_W, NA_WIN_ROWS * GRID_W)
    return jnp.concatenate([dense, jnp.full_like(dense[:, :1], MASK_BIAS)], axis=1).astype(F32)


def _out_kernel(x_ref, yf_ref, yb_ref, bonus_ref, ya_ref, yn_ref, g_ref, mod_ref, w_ref,
                gng_ref, gnb_ref, lng_ref, lnb_ref, ones_ref, o_ref, *, alpha):
    ones = ones_ref[...]
    ys = yf_ref[...] + yb_ref[...]
    mu = _dot(ys, ones, HI) * (1.0 / HEAD_DIM)
    dv = ys - mu
    var = _dot(dv * dv, ones, HI) * (1.0 / HEAD_DIM)
    yr = dv * lax.rsqrt(var + GN_EPS) * gng_ref[...] + gnb_ref[...] + bonus_ref[...]
    g = g_ref[...].astype(F32)
    parts = (
        yr * _silu(g[:, 0:RWKV_W]),
        ya_ref[...] * _silu(g[:, RWKV_W:RWKV_W + GQA_W]),
        yn_ref[...] * _silu(g[:, RWKV_W + GQA_W:]),
    )
    mixed = jnp.concatenate([p.astype(BF16) for p in parts], axis=1)
    y = _dot(mixed, w_ref[...])
    xn = alpha * x_ref[...] + mod_ref[2:3, :] * y
    mean = jnp.mean(xn, axis=-1, keepdims=True)
    xc = xn - mean
    var = jnp.mean(xc * xc, axis=-1, keepdims=True)
    o_ref[...] = xc * lax.rsqrt(var + LN_EPS) * lng_ref[...] + lnb_ref[...]


def _output(xc, yf, yb, bonus, ya, yn, gates, modsel, w_out, gn_g, gn_b, ln_g, ln_b, ones_heads, alpha, n_ctx_tiles,
            skip_tiles):
    b, t, d = xc.shape
    nt = t // TOKEN_TILE - skip_tiles
    tok = lambda width: pl.BlockSpec((None, TOKEN_TILE, width), lambda bi, i: (bi, i + skip_tiles, 0))
    full = lambda shape: pl.BlockSpec(shape, lambda bi, i: (0,) * len(shape))
    return pl.pallas_call(
        functools.partial(_out_kernel, alpha=alpha),
        grid=(b, nt),
        in_specs=[
            tok(d), tok(RWKV_W), tok(RWKV_W), tok(RWKV_W), tok(GQA_W), tok(NA_W), tok(RWKV_W + GQA_W + NA_W),
            pl.BlockSpec((None, None, 3, d), lambda bi, i: (bi, jnp.where(i + skip_tiles < n_ctx_tiles, 0, 1), 0, 0)),
            full(w_out.shape), full((1, RWKV_W)), full((1, RWKV_W)), full((1, d)), full((1, d)), full((RWKV_W, RWKV_W)),
        ],
        out_specs=pl.BlockSpec((None, TOKEN_TILE, d), lambda bi, i: (bi, i, 0)),
        out_shape=jax.ShapeDtypeStruct((b, nt * TOKEN_TILE, d), F32),
        compiler_params=_cparams("parallel", "parallel"),
        name="out",
    )(xc, yf, yb, bonus, ya, yn, gates, modsel, w_out, gn_g, gn_b, ln_g, ln_b, ones_heads)


def _rope_tables(n_ctx, n_lat):
    t = jnp.arange(n_lat, dtype=jnp.int32)
    row = (t // GRID_W).astype(F32)
    col = (t % GRID_W).astype(F32)
    n_freq = HEAD_DIM // 4
    inv_freq = ROPE_THETA ** (-jnp.arange(n_freq, dtype=F32) / n_freq)
    ang_r = row[:, None] * inv_freq
    ang_c = col[:, None] * inv_freq
    ang = jnp.concatenate([ang_r, ang_r, ang_c, ang_c], axis=-1)
    cos = jnp.concatenate([jnp.ones((n_ctx, HEAD_DIM), F32), jnp.cos(ang)], axis=0)
    sin = jnp.concatenate([jnp.zeros((n_ctx, HEAD_DIM), F32), jnp.sin(ang)], axis=0)
    first = (np.arange(HEAD_DIM) % (HEAD_DIM // 2)) < HEAD_DIM // 4
    sin_a = jnp.where(first, -sin, 0.0)
    sin_b = jnp.where(first, 0.0, sin)
    pair = lambda z: jnp.concatenate([z, z], axis=-1)
    return pair(cos), pair(sin_a), pair(sin_b)


def _block_ones(n_heads):
    blk = np.kron(np.eye(n_heads, dtype=np.float32), np.ones((HEAD_DIM, HEAD_DIM), np.float32))
    return jnp.asarray(blk)


def kernel(x, c, ctx, c_ctx, w_mod, b_mod, w_in, w_out, rwkv_conv, decay_w0, decay_w2, iclr_a0, iclr_a2, rwkv_k_k, rwkv_k_a, rwkv_r_k, rwkv_gn_g, rwkv_gn_b, gqa_q_norm, gqa_k_norm, na_rpb, ln_g, ln_b):
    b, n_lat, d = x.shape
    n_ctx = ctx.shape[1]
    depth = w_mod.shape[0]
    alpha = float((2 * depth) ** 0.25)
    n_ctx_tiles = n_ctx // TOKEN_TILE
    assert n_ctx % TOKEN_TILE == 0 and n_lat % TOKEN_TILE == 0 and n_lat % GRID_W == 0

    n_rows = -(-(b + 1) // SUBLANES) * SUBLANES
    cvec = jnp.zeros((n_rows, d), F32).at[:b].set(c).at[b].set(c_ctx)
    mod = _modulation(cvec, w_mod, b_mod).reshape(depth, n_rows, 3, d)
    mod_ctx = jnp.broadcast_to(mod[:, b][:, None], (depth, b, 3, d))
    modsel = jnp.stack([mod_ctx, mod[:, :b]], axis=2)

    sizes = (RWKV_CONV_W, RWKV_W, GQA_W, GQA_KV_W, GQA_KV_W, GQA_W, NA_W, NA_W, NA_W, NA_W)
    offs = np.concatenate([[0], np.cumsum(sizes)])
    seg = lambda n: np.arange(offs[n], offs[n + 1])
    head_perm = np.concatenate([np.arange(h * HEAD_DIM, (h + 1) * HEAD_DIM) for h in (0, 3, 1, 4, 2, 5)])
    cols = np.concatenate([seg(0), seg(2)[head_perm], seg(3), seg(4), seg(6), seg(7), seg(8),
                           seg(1), seg(5)[head_perm], seg(9)])
    w_in_p = w_in[:, :, cols].astype(BF16)
    out_rows = np.concatenate([np.arange(RWKV_W), RWKV_W + head_perm, np.arange(RWKV_W + GQA_W, RWKV_W + GQA_W + NA_W)])
    w_out_p = w_out[:, out_rows, :].astype(BF16)

    w = RWKV_W
    wlr = jnp.zeros((depth, LANES, 4 * w), F32)
    for dd in range(2):
        wlr = wlr.at[:, dd * LOW_RANK:(dd + 1) * LOW_RANK, dd * w:(dd + 1) * w].set(decay_w2[:, dd])
        wlr = wlr.at[:, (2 + dd) * LOW_RANK:(3 + dd) * LOW_RANK, (2 + dd) * w:(3 + dd) * w].set(iclr_a2[:, dd])
    b0 = jnp.concatenate([decay_w0[:, 0], decay_w0[:, 1], iclr_a0[:, 0], iclr_a0[:, 1]], axis=-1)[:, None, :]

    cos, sin_a, sin_b = _rope_tables(n_ctx, n_lat)
    ones_heads = _block_ones(RWKV_HEADS)
    ones_pair = _block_ones(2)
    pair = lambda z: jnp.concatenate([z, z], axis=-1)[:, None, :]
    qn, kn = pair(gqa_q_norm), pair(gqa_k_norm)

    xc = jnp.concatenate([ctx, x], axis=1)
    t = n_ctx + n_lat
    kv_chunk = next(ch for ch in KV_CHUNKS if t % ch == 0)
    for l in range(depth):
        ru, aq, ak, av, nq4, nk, nv, gates = _projection(xc, modsel[l], w_in_p[l], n_ctx_tiles)
        prep = _rwkv_prep(ru, rwkv_conv[l], wlr[l], b0[l], rwkv_k_k[l][None], rwkv_k_a[l][None],
                          rwkv_r_k[l].reshape(1, w), ones_heads, n_ctx_tiles)
        yf, yb = _rwkv_scan(prep, n_ctx // SCAN_CHUNK)
        last = l == depth - 1
        qt, kr, vt4 = _qk_prep(aq, ak, av, cos, sin_a, sin_b, qn[l], kn[l], ones_pair, kv_chunk)
        k4 = kr.reshape(b, t // kv_chunk, kv_chunk, GQA_KV_W)
        ya = _gqa(qt, k4, vt4, n_ctx, n_lat)
        if last:
            ya = jnp.concatenate([jnp.zeros((b, n_ctx, GQA_W), F32), ya], axis=1)
        else:
            ya_c = _gqa(qt, kr[:, :n_ctx].reshape(b, 1, n_ctx, GQA_KV_W), vt4[:, :1, :, :n_ctx], 0, n_ctx)
            ya = jnp.concatenate([ya_c, ya], axis=1)
        yn = _natten(nq4, nk, nv, _natten_bias(na_rpb[l]), n_ctx)
        xc = _output(xc, yf, yb, prep[9], ya, yn, gates, modsel[l], w_out_p[l], rwkv_gn_g[l][None], rwkv_gn_b[l][None],
                     ln_g[l][None], ln_b[l][None], ones_heads, alpha, n_ctx_tiles, n_ctx_tiles if last else 0)
    return xc
```

```python
import functools

import numpy as np
import jax
import jax.numpy as jnp
from jax import lax
from jax.experimental import pallas as pl
from jax.experimental.pallas import tpu as pltpu

F32 = jnp.float32
BF16 = jnp.bfloat16
HI = lax.Precision.HIGHEST

HEAD_DIM = 64
RWKV_HEADS = 6
GQA_Q_HEADS = 6
GQA_KV_HEADS = 2
NA_HEADS = 4
RWKV_W = RWKV_HEADS * HEAD_DIM
GQA_W = GQA_Q_HEADS * HEAD_DIM
GQA_KV_W = GQA_KV_HEADS * HEAD_DIM
NA_W = NA_HEADS * HEAD_DIM
LOW_RANK = 32
RWKV_CONV_W = 3 * RWKV_W + 4 * LOW_RANK
GRID_W = 64
NA_WIN_ROWS = 8
NA_WIN_COLS = 16
ROPE_THETA = 10000.0
LN_EPS = 1e-5
RMS_EPS = 1e-6
GN_EPS = 64e-5
KK_EPS = 1e-12
MASK_BIAS = -1e30

LANES = 128
SUBLANES = 8
TOKEN_TILE = 256
SCAN_CHUNK = 64
KV_CHUNKS = (768, 512, 256)
GQA_Q_TILES = (512, 256)
LOG2_E = 1.4426950408889634
VMEM_LIMIT = 56 * 1024 * 1024


def _cparams(*sem):
    return pltpu.CompilerParams(dimension_semantics=sem, vmem_limit_bytes=VMEM_LIMIT)


def _dot(a, b, precision=None):
    return jnp.dot(a, b, preferred_element_type=F32, precision=precision)


def _dot_nt(a, b, precision=None):
    return lax.dot_general(a, b, (((1,), (1,)), ((), ())), preferred_element_type=F32, precision=precision)


def _dot_tn(a, b, precision=None):
    return lax.dot_general(a, b, (((0,), (0,)), ((), ())), preferred_element_type=F32, precision=precision)


def _sigmoid(x):
    return 1.0 / (1.0 + jnp.exp(-x))


def _silu(x):
    return x * _sigmoid(x)


def _mod_kernel(c_ref, w_ref, b_ref, o_ref):
    o_ref[...] = _dot(_silu(c_ref[...]), w_ref[...], HI) + b_ref[...]


def _modulation(cvec, w_mod, b_mod):
    depth, d, d3 = w_mod.shape
    r = cvec.shape[0]
    nj = d3 // d
    return pl.pallas_call(
        _mod_kernel,
        grid=(depth, nj),
        in_specs=[
            pl.BlockSpec((r, d), lambda l, j: (0, 0)),
            pl.BlockSpec((None, d, d), lambda l, j: (l, 0, j)),
            pl.BlockSpec((None, 1, d), lambda l, j: (l, 0, j)),
        ],
        out_specs=pl.BlockSpec((None, r, d), lambda l, j: (l, 0, j)),
        out_shape=jax.ShapeDtypeStruct((depth, r, d3), F32),
        compiler_params=_cparams("parallel", "parallel"),
        name="modulation",
    )(cvec, w_mod, b_mod.reshape(depth, 1, d3))


_PROJ_SLABS = (
    ("ru", RWKV_CONV_W, F32),
    ("aq", GQA_W, F32),
    ("ak", GQA_KV_W, F32),
    ("av", GQA_KV_W, BF16),
    ("nq", NA_W, None),
    ("nk", NA_W, BF16),
    ("nv", NA_W, BF16),
    ("gates", RWKV_W + GQA_W + NA_W, BF16),
)


def _proj_kernel(x_ref, mod_ref, w_ref, ru_ref, aq_ref, ak_ref, av_ref, nq_ref, nk_ref, nv_ref, g_ref):
    shift = mod_ref[0:1, :]
    scale = mod_ref[1:2, :]
    h = (x_ref[...] * (1.0 + scale) + shift).astype(BF16)
    outs = {"ru": ru_ref, "aq": aq_ref, "ak": ak_ref, "av": av_ref, "nk": nk_ref, "nv": nv_ref, "gates": g_ref}
    off = 0
    for name, width, _ in _PROJ_SLABS:
        u = _dot(h, w_ref[:, off:off + width])
        if name == "nq":
            u = u * (HEAD_DIM ** -0.5)
            lane = lax.broadcasted_iota(jnp.int32, (u.shape[0], LANES), 1)
            for hd in range(NA_HEADS):
                pair = u[:, (hd // 2) * LANES:(hd // 2 + 1) * LANES]
                keep = (lane < HEAD_DIM) if hd % 2 == 0 else (lane >= HEAD_DIM)
                nq_ref[:, hd * LANES:(hd + 1) * LANES] = jnp.where(keep, pair, 0.0).astype(BF16)
        else:
            outs[name][...] = u.astype(outs[name].dtype)
        off += width


def _projection(xc, modsel, w_perm, n_ctx_tiles):
    b, t, d = xc.shape
    nt = t // TOKEN_TILE
    n_in = w_perm.shape[1]
    shapes, specs = [], []
    for name, width, dt in _PROJ_SLABS:
        if name == "nq":
            width, dt = NA_HEADS * LANES, BF16
        shapes.append(jax.ShapeDtypeStruct((b, t, width), dt))
        specs.append(pl.BlockSpec((None, TOKEN_TILE, width), lambda bi, i: (bi, i, 0)))
    return pl.pallas_call(
        _proj_kernel,
        grid=(b, nt),
        in_specs=[
            pl.BlockSpec((None, TOKEN_TILE, d), lambda bi, i: (bi, i, 0)),
            pl.BlockSpec((None, None, 3, d), lambda bi, i: (bi, jnp.where(i < n_ctx_tiles, 0, 1), 0, 0)),
            pl.BlockSpec((d, n_in), lambda bi, i: (0, 0)),
        ],
        out_specs=specs,
        out_shape=shapes,
        compiler_params=_cparams("parallel", "parallel"),
        name="proj",
    )(xc, modsel, w_perm)


def _rwkv_prep_kernel(x_ref, xp_ref, xn_ref, cw_ref, wlr_ref, b0_ref, kk_ref, ka_ref, rk_ref, ones_ref,
                      r_o, v_o, kk_o, lw0_o, kd0_o, bd0_o, lw1_o, kd1_o, bd1_o, bonus_o, pad_ref, *, n_ctx_tiles):
    i = pl.program_id(1)
    nt = pl.num_programs(1)
    tm = x_ref.shape[0]
    prev_ok = jnp.logical_and(i != 0, i != n_ctx_tiles)
    next_ok = jnp.logical_and(i != n_ctx_tiles - 1, i != nt - 1)
    pad_ref[0:SUBLANES, :] = jnp.where(prev_ok, xp_ref[...], 0.0)
    pad_ref[SUBLANES:SUBLANES + tm, :] = x_ref[...]
    pad_ref[SUBLANES + tm:2 * SUBLANES + tm, :] = jnp.where(next_ok, xn_ref[...], 0.0)
    x_prev = pad_ref[pl.ds(SUBLANES - 1, tm), :]
    x_next = pad_ref[pl.ds(SUBLANES + 1, tm), :]
    u = cw_ref[0:1, :] * x_prev + cw_ref[1:2, :] * x_ref[...] + cw_ref[2:3, :] * x_next

    w = RWKV_W
    r, k, v = u[:, 0:w], u[:, w:2 * w], u[:, 2 * w:3 * w]
    lr = u[:, 3 * w:3 * w + LANES]
    lane = lax.broadcasted_iota(jnp.int32, lr.shape, 1)
    z = jnp.where(lane < 2 * LOW_RANK, jnp.tanh(lr), lr)
    z_hi, z_lo = _split2(z)
    pre = _dot(jnp.concatenate([z_hi, z_lo, z_hi], axis=1), wlr_ref[...]) + b0_ref[...]

    ones = ones_ref[...]
    kk0 = k * kk_ref[...]
    ss = _head_sums(kk0 * kk0, ones)
    kk = kk0 * lax.rsqrt(jnp.maximum(ss, KK_EPS))
    r_o[...] = r
    v_o[...] = v
    kk_o[...] = kk
    kd_sum = None
    for d, (lw_o, kd_o, bd_o) in enumerate(((lw0_o, kd0_o, bd0_o), (lw1_o, kd1_o, bd1_o))):
        xd = -pre[:, d * w:(d + 1) * w]
        softplus = jnp.maximum(xd, 0.0) + jnp.log(1.0 + jnp.exp(-jnp.abs(xd)))
        lw_o[...] = -jnp.exp(-softplus - 0.5)
        iclr = _sigmoid(pre[:, (2 + d) * w:(3 + d) * w])
        kd = k * (1.0 + (iclr - 1.0) * ka_ref[...])
        kd_o[...] = kd
        bd_o[...] = kk * iclr
        kd_sum = kd if kd_sum is None else kd_sum + kd
    bonus_o[...] = _head_sums(r * kd_sum * rk_ref[...], ones) * v


def _rwkv_prep(ru, conv_w, wlr, b0, k_k, k_a, r_k, ones_heads, n_ctx_tiles):
    b, t, cw = ru.shape
    nt = t // TOKEN_TILE
    hb = TOKEN_TILE // SUBLANES
    w = RWKV_W
    full = lambda shape: pl.BlockSpec(shape, lambda bi, i: (0,) * len(shape))
    out_spec = pl.BlockSpec((None, TOKEN_TILE, w), lambda bi, i: (bi, i, 0))
    return pl.pallas_call(
        functools.partial(_rwkv_prep_kernel, n_ctx_tiles=n_ctx_tiles),
        grid=(b, nt),
        in_specs=[
            pl.BlockSpec((None, TOKEN_TILE, cw), lambda bi, i: (bi, i, 0)),
            pl.BlockSpec((None, SUBLANES, cw), lambda bi, i: (bi, jnp.maximum(i * hb - 1, 0), 0)),
            pl.BlockSpec((None, SUBLANES, cw), lambda bi, i: (bi, jnp.minimum((i + 1) * hb, t // SUBLANES - 1), 0)),
            full((3, cw)), full((3 * LANES, 4 * w)), full((1, 4 * w)),
            full((1, w)), full((1, w)), full((1, w)), full((2 * w, w)),
        ],
        out_specs=[out_spec] * 10,
        out_shape=[jax.ShapeDtypeStruct((b, t, w), F32)] * 10,
        scratch_shapes=[pltpu.VMEM((TOKEN_TILE + 2 * SUBLANES, cw), F32)],
        compiler_params=_cparams("parallel", "parallel"),
        name="rwkv_prep",
    )(ru, ru, ru, conv_w, wlr, b0, k_k, k_a, r_k, ones_heads)


def _split2(x):
    hi = x.astype(BF16)
    return hi, (x - hi.astype(F32)).astype(BF16)


def _head_sums(x, ones2):
    hi, lo = _split2(x)
    return _dot(jnp.concatenate([hi, lo], axis=1), ones2)


def _b(x):
    return x.astype(BF16)


def _bdot(a, b):
    return lax.dot_general(a, b, (((2,), (1,)), ((0,), (0,))), preferred_element_type=F32)


def _bdot_nt(a, b):
    return lax.dot_general(a, b, (((2,), (2,)), ((0,), (0,))), preferred_element_type=F32)


def _bdot_tn(a, b):
    return lax.dot_general(a, b, (((1,), (1,)), ((0,), (0,))), preferred_element_type=F32)


def _chunk_step(r, k, v, kap, bet, lw, h0, consts):
    tri, rev, strict, incl, head_rows, eye2, eye_l = consts
    c = r.shape[1]
    l1 = lw.astype(BF16)
    l2 = (lw - l1.astype(F32)).astype(BF16)
    l3 = (lw - l1.astype(F32) - l2.astype(F32)).astype(BF16)
    lcs = _bdot(tri, jnp.concatenate([l1, l2, l3], axis=2))
    lc = lcs[:, :, 0:LANES] + lcs[:, :, LANES:2 * LANES] + lcs[:, :, 2 * LANES:3 * LANES]
    ltot = jnp.where(rev, lc[:, 0:1, :], lc[:, c - 1:c, :])
    e_in = jnp.exp(lc)
    e_inv = jnp.exp(-lc)
    e_hat = jnp.exp(ltot - lc)

    def stack(x):
        return jnp.where(head_rows, jnp.concatenate([x, x], axis=1), 0.0)

    r_s = stack(r * e_in)
    kap_s = stack(kap * jnp.exp(lc - lw))
    k_s = stack(k * e_inv)
    b_s = stack(bet * e_inv)
    kh_s = stack(k * e_hat)
    bh_s = stack(bet * e_hat)
    v_s = stack(v)

    m = _bdot_nt(_b(jnp.concatenate([kap_s, r_s], axis=1)), _b(jnp.concatenate([b_s, k_s], axis=1)))
    c2 = 2 * c
    a = jnp.where(strict, m[:, 0:c2, 0:c2], 0.0)
    bm = jnp.where(strict, m[:, 0:c2, c2:2 * c2], 0.0)
    mb = jnp.where(incl, m[:, c2:2 * c2, 0:c2], 0.0)
    mk = jnp.where(incl, m[:, c2:2 * c2, c2:2 * c2], 0.0)

    t_inv = eye2 - a
    pw = a
    for _ in range(int(np.log2(c)) - 1):
        pw_b = _b(_bdot(_b(pw), _b(pw)))
        pw = pw_b
        t_inv = t_inv + _bdot(_b(t_inv), pw_b)

    v_b = _b(v_s)
    bv = _bdot(_b(bm), v_b)
    x_b = _b(_bdot(_b(t_inv), _b(jnp.concatenate([kap_s, bv], axis=2))))
    z = _bdot(_b(mb), x_b)
    rp = r_s - z[:, :, 0:LANES]
    y0 = _bdot(_b(mk), v_b) - z[:, :, LANES:2 * LANES]
    bhx = _bdot_tn(_b(bh_s), x_b)
    g = jnp.where(eye_l, jnp.exp(ltot), 0.0) - bhx[:, :, 0:LANES]
    hadd = _bdot_tn(_b(kh_s), v_b) - bhx[:, :, LANES:2 * LANES]

    h_b = _b(h0)
    ys = _bdot(_b(rp), h_b) + y0
    hn = _bdot(_b(g), h_b) + hadd
    return ys[:, 0:c, :] + ys[:, c:c2, :], hn


def _rwkv_scan_kernel(rf, vf, kkf, lwf, kdf, bdf, rb, vb, kkb, lwb, kdb, bdb, yf_ref, yb_ref, h_ref):
    i = pl.program_id(1)

    @pl.when(i == 0)
    def _():
        h_ref[...] = jnp.zeros_like(h_ref)

    c = rf.shape[0]
    c2 = 2 * c
    n_pairs = RWKV_W // LANES
    n = 2 * n_pairs
    iota = lambda shape, dim: lax.broadcasted_iota(jnp.int32, shape, dim)
    rev = iota((n, 1, 1), 0) >= n_pairs
    sign = lambda shape: jnp.where(iota(shape, 0) >= n_pairs, 1, -1)
    tri = jnp.where((iota((n, c, c), 2) - iota((n, c, c), 1)) * sign((n, c, c)) >= 0, 1.0, 0.0).astype(BF16)
    row2, col2 = iota((n, c2, c2), 1), iota((n, c2, c2), 2)
    same = (row2 < c) == (col2 < c)
    rt = jnp.where(row2 < c, row2, row2 - c)
    ct = jnp.where(col2 < c, col2, col2 - c)
    before = (ct - rt) * sign((n, c2, c2))
    strict = jnp.logical_and(same, before > 0)
    incl = jnp.logical_and(same, before >= 0)
    head_rows = (iota((n, c2, LANES), 1) < c) == (iota((n, c2, LANES), 2) < HEAD_DIM)
    eye2 = jnp.where(row2 == col2, 1.0, 0.0)
    eye_l = iota((n, LANES, LANES), 1) == iota((n, LANES, LANES), 2)
    consts = (tri, rev, strict, incl, head_rows, eye2, eye_l)

    def chains(fref, bref):
        return jnp.stack([ref[:, p * LANES:(p + 1) * LANES] for ref in (fref, bref) for p in range(n_pairs)], axis=0)

    y, hn = _chunk_step(chains(rf, rb), chains(kdf, kdb), chains(vf, vb), chains(kkf, kkb), chains(bdf, bdb),
                        chains(lwf, lwb), h_ref[...], consts)
    h_ref[...] = hn
    for p in range(n_pairs):
        yf_ref[:, p * LANES:(p + 1) * LANES] = y[p]
        yb_ref[:, p * LANES:(p + 1) * LANES] = y[n_pairs + p]


def _rwkv_scan(prep, n_ctx_chunks):
    r, v, kk, lw0, kd0, bd0, lw1, kd1, bd1, _ = prep
    b, t, w = r.shape
    nc = t // SCAN_CHUNK
    ncc = n_ctx_chunks

    def bwd_chunk(i):
        return jnp.where(i < ncc, ncc - 1 - i, nc - 1 + ncc - i)

    fspec = pl.BlockSpec((None, SCAN_CHUNK, w), lambda bi, i: (bi, i, 0))
    bspec = pl.BlockSpec((None, SCAN_CHUNK, w), lambda bi, i: (bi, bwd_chunk(i), 0))
    return pl.pallas_call(
        _rwkv_scan_kernel,
        grid=(b, nc),
        in_specs=[fspec] * 6 + [bspec] * 6,
        out_specs=[fspec, bspec],
        out_shape=[jax.ShapeDtypeStruct((b, t, w), F32)] * 2,
        scratch_shapes=[pltpu.VMEM((2 * (w // LANES), LANES, LANES), F32)],
        compiler_params=_cparams("parallel", "arbitrary"),
        name="rwkv_scan",
    )(r, v, kk, lw0, kd0, bd0, r, v, kk, lw1, kd1, bd1)


def _qk_prep_kernel(aq_ref, ak_ref, av_ref, cos_ref, sa_ref, sb_ref, qn_ref, kn_ref, ones_ref, qt_o, k_o, vt_o):
    cos, sin_a, sin_b = cos_ref[...], sa_ref[...], sb_ref[...]
    ones = ones_ref[...]
    tm = cos.shape[0]
    row = lax.broadcasted_iota(jnp.int32, (LANES, tm), 0)

    def norm_rope(x, g):
        ms = _head_sums(x * x, ones) * (1.0 / HEAD_DIM)
        xn = x * lax.rsqrt(ms + RMS_EPS) * g
        return (xn * cos + pltpu.roll(xn, LANES - HEAD_DIM // 4, 1) * sin_a
                + pltpu.roll(xn, HEAD_DIM // 4, 1) * sin_b)

    for j in range(GQA_W // LANES):
        q = norm_rope(aq_ref[:, j * LANES:(j + 1) * LANES], qn_ref[...]) * (HEAD_DIM ** -0.5 * LOG2_E)
        qt = q.T
        qt_o[(2 * j) * LANES:(2 * j + 1) * LANES, :] = jnp.where(row < HEAD_DIM, qt, 0.0).astype(BF16)
        qt_o[(2 * j + 1) * LANES:(2 * j + 2) * LANES, :] = jnp.where(row >= HEAD_DIM, qt, 0.0).astype(BF16)
    k_o[...] = norm_rope(ak_ref[...], kn_ref[...]).astype(BF16)
    vt = av_ref[...].astype(F32).T
    one = jnp.ones((HEAD_DIM, tm), F32)
    vt_o[...] = jnp.concatenate([vt[0:HEAD_DIM], one, vt[HEAD_DIM:2 * HEAD_DIM], one], axis=0).astype(BF16)


def _qk_prep(aq, ak, av, cos, sin_a, sin_b, qn, kn, ones_pair, kv_chunk, n_ctx_tiles):
    b, t, _ = aq.shape
    nt = t // TOKEN_TILE
    per = kv_chunk // TOKEN_TILE
    tok = lambda width: pl.BlockSpec((None, TOKEN_TILE, width), lambda bi, i: (bi, i, 0))
    tab = pl.BlockSpec((TOKEN_TILE, LANES), lambda bi, i: (i, 0))
    full = lambda shape: pl.BlockSpec(shape, lambda bi, i: (0,) * len(shape))
    return pl.pallas_call(
        _qk_prep_kernel,
        grid=(b, nt),
        in_specs=[tok(GQA_W), tok(GQA_KV_W), tok(GQA_KV_W), tab, tab, tab,
                  full((1, LANES)), full((1, LANES)), full((2 * LANES, LANES))],
        out_specs=[
            pl.BlockSpec((None, 2 * GQA_W, TOKEN_TILE),
                         lambda bi, i: (bi, 0, jnp.where(i < n_ctx_tiles, i + nt - n_ctx_tiles, i - n_ctx_tiles))),
            tok(GQA_KV_W),
            pl.BlockSpec((None, None, 2 * LANES, TOKEN_TILE), lambda bi, i: (bi, i // per, 0, i % per)),
        ],
        out_shape=[
            jax.ShapeDtypeStruct((b, 2 * GQA_W, t), BF16),
            jax.ShapeDtypeStruct((b, t, GQA_KV_W), BF16),
            jax.ShapeDtypeStruct((b, t // kv_chunk, 2 * LANES, kv_chunk), BF16),
        ],
        compiler_params=_cparams("parallel", "parallel"),
        name="qk_prep",
    )(aq, ak, av, cos, sin_a, sin_b, qn, kn, ones_pair)


def _gqa_kernel(qt_ref, k_ref, vt_ref, o_ref, acc_ref):
    n_chunks = k_ref.shape[0]
    tq = qt_ref.shape[1]
    nj = GQA_W // LANES
    acc_ref[...] = jnp.zeros_like(acc_ref)

    def body(c, ms):
        kc = k_ref[c]
        new_ms = []
        for g in range(GQA_KV_HEADS):
            qt = jnp.concatenate([qt_ref[(2 * j + g) * LANES:(2 * j + g + 1) * LANES, :] for j in range(nj)], axis=1)
            s = _dot(kc, qt)
            vt = vt_ref[c, g * LANES:(g + 1) * LANES, :]
            m_new = jnp.maximum(ms[g], jnp.max(s, axis=0, keepdims=True))
            alpha = jnp.exp2(ms[g] - m_new)
            p = jnp.exp2(s - m_new).astype(BF16)
            for j in range(nj):
                sl = slice(j * tq, (j + 1) * tq)
                acc_ref[g, j] = alpha[:, sl] * acc_ref[g, j] + _dot(vt, p[:, sl])
            new_ms.append(m_new)
        return tuple(new_ms)

    m0 = jnp.full((1, nj * tq), -jnp.inf, F32)
    lax.fori_loop(0, n_chunks, body, (m0,) * GQA_KV_HEADS)
    for j in range(nj):
        outs = []
        for g in range(GQA_KV_HEADS):
            acc = acc_ref[g, j]
            outs.append(acc[0:HEAD_DIM] / acc[HEAD_DIM:HEAD_DIM + 1])
        o_ref[:, j * LANES:(j + 1) * LANES] = jnp.concatenate(outs, axis=0).T


def _gqa(qt, k4, vt4, q_start, n_q):
    b = qt.shape[0]
    _, n_chunks, kv_chunk, _ = k4.shape
    tq = next(w for w in GQA_Q_TILES if n_q % w == 0 and q_start % w == 0)
    q0 = q_start // tq
    return pl.pallas_call(
        _gqa_kernel,
        grid=(b, n_q // tq),
        in_specs=[
            pl.BlockSpec((None, 2 * GQA_W, tq), lambda bi, i: (bi, 0, i + q0)),
            pl.BlockSpec((None, n_chunks, kv_chunk, GQA_KV_W), lambda bi, i: (bi, 0, 0, 0)),
            pl.BlockSpec((None, n_chunks, 2 * LANES, kv_chunk), lambda bi, i: (bi, 0, 0, 0)),
        ],
        out_specs=pl.BlockSpec((None, tq, GQA_W), lambda bi, i: (bi, i, 0)),
        out_shape=jax.ShapeDtypeStruct((b, n_q, GQA_W), F32),
        scratch_shapes=[pltpu.VMEM((GQA_KV_HEADS, GQA_W // LANES, LANES, tq), F32)],
        compiler_params=_cparams("parallel", "parallel"),
        name="gqa",
    )(qt, k4, vt4)


def _natten_kernel(q_ref, k_ref, v_ref, bias_ref, o_ref, *, n_ctx, rows):
    i = pl.program_id(1)
    nq = GRID_W
    rows_per_step = q_ref.shape[0] // nq
    win = NA_WIN_ROWS * GRID_W
    is_ctx = i * rows_per_step < n_ctx // nq
    n_pairs = NA_W // LANES
    qs, kws, vws, kcs, vcs, bs = [], [], [], [], [], []
    for rr in range(rows_per_step):
        r = i * rows_per_step + rr - n_ctx // nq
        r_start = jnp.clip(r - NA_WIN_ROWS // 2, 0, rows - NA_WIN_ROWS)
        off = jnp.where(is_ctx, NA_WIN_ROWS, r_start - r + NA_WIN_ROWS - 1)
        koff = pl.multiple_of(n_ctx + r_start * GRID_W, GRID_W)
        for pair in range(n_pairs):
            sl = slice(pair * LANES, (pair + 1) * LANES)
            kw, vw = k_ref[pl.ds(koff, win), sl], v_ref[pl.ds(koff, win), sl]
            kc, vc = k_ref[0:n_ctx, sl], v_ref[0:n_ctx, sl]
            for half in range(2):
                hd = 2 * pair + half
                qs.append(q_ref[rr * nq:(rr + 1) * nq, hd * LANES:(hd + 1) * LANES])
                bs.append(bias_ref[hd, off])
                kws.append(kw)
                vws.append(vw)
                kcs.append(kc)
                vcs.append(vc)
    q = jnp.stack(qs, axis=0)
    s_w = _bdot_nt(q, jnp.stack(kws, axis=0)) + jnp.stack(bs, axis=0)
    s_c = _bdot_nt(q, jnp.stack(kcs, axis=0))
    m = jnp.maximum(jnp.max(s_w, axis=-1, keepdims=True), jnp.max(s_c, axis=-1, keepdims=True))
    p_w = jnp.exp(s_w - m)
    p_c = jnp.exp(s_c - m)
    l = jnp.sum(p_w, axis=-1, keepdims=True) + jnp.sum(p_c, axis=-1, keepdims=True)
    o = (_bdot(p_w.astype(BF16), jnp.stack(vws, axis=0)) + _bdot(p_c.astype(BF16), jnp.stack(vcs, axis=0))) / l
    lane = lax.broadcasted_iota(jnp.int32, (nq, LANES), 1)
    for rr in range(rows_per_step):
        for pair in range(n_pairs):
            e = (rr * n_pairs + pair) * 2
            o_ref[rr * nq:(rr + 1) * nq, pair * LANES:(pair + 1) * LANES] = jnp.where(lane < HEAD_DIM, o[e], o[e + 1])


def _natten(nq4, nk, nv, bias, n_ctx):
    b, t, _ = nk.shape
    rows = (t - n_ctx) // GRID_W
    assert rows >= NA_WIN_ROWS and n_ctx % TOKEN_TILE == 0 and TOKEN_TILE % GRID_W == 0
    kv = pl.BlockSpec((None, t, NA_W), lambda bi, i: (bi, 0, 0))
    return pl.pallas_call(
        functools.partial(_natten_kernel, n_ctx=n_ctx, rows=rows),
        grid=(b, t // TOKEN_TILE),
        in_specs=[
            pl.BlockSpec((None, TOKEN_TILE, NA_HEADS * LANES), lambda bi, i: (bi, i, 0)),
            kv, kv,
            pl.BlockSpec(bias.shape, lambda bi, i: (0, 0, 0, 0)),
        ],
        out_specs=pl.BlockSpec((None, TOKEN_TILE, NA_W), lambda bi, i: (bi, i, 0)),
        out_shape=jax.ShapeDtypeStruct((b, t, NA_W), F32),
        compiler_params=_cparams("parallel", "parallel"),
        name="natten",
    )(nq4, nk, nv, bias)


def _natten_bias(rpb):
    cols = np.arange(GRID_W)
    c_start = np.clip(cols - NA_WIN_COLS // 2, 0, GRID_W - NA_WIN_COLS)
    key = np.arange(GRID_W)
    valid = (key[None, :] >= c_start[:, None]) & (key[None, :] < c_start[:, None] + NA_WIN_COLS)
    dc = key[None, :] - cols[:, None] + NA_WIN_COLS - 1
    place = (valid[:, :, None] & (dc[:, :, None] == np.arange(2 * NA_WIN_COLS - 1))).astype(np.float32)
    mask = np.where(valid, 0.0, MASK_BIAS).astype(np.float32)
    rows = jnp.stack([rpb[:, o:o + NA_WIN_ROWS, :] for o in range(NA_WIN_ROWS)], axis=1)
    dense = jnp.einsum("hopd,cxd->hocpx", rows, jnp.asarray(place), precision=HI) + mask[None, None, :, None, :]
    dense = dense.reshape(NA_HEADS, NA_WIN_ROWS, GRID_W, NA_WIN_ROWS * GRID_W)
    return jnp.concatenate([dense, jnp.full_like(dense[:, :1], MASK_BIAS)], axis=1).astype(F32)


def _out_kernel(x_ref, yf_ref, yb_ref, bonus_ref, ya_ref, yn_ref, g_ref, mod_ref, w_ref,
                gng_ref, gnb_ref, lng_ref, lnb_ref, ones_ref, o_ref, *, alpha):
    ones = ones_ref[...]
    ys = yf_ref[...] + yb_ref[...]
    mu = _head_sums(ys, ones) * (1.0 / HEAD_DIM)
    dv = ys - mu
    var = _head_sums(dv * dv, ones) * (1.0 / HEAD_DIM)
    yr = dv * lax.rsqrt(var + GN_EPS) * gng_ref[...] + gnb_ref[...] + bonus_ref[...]
    g = g_ref[...].astype(F32)
    parts = (
        yr * _silu(g[:, 0:RWKV_W]),
        ya_ref[...] * _silu(g[:, RWKV_W:RWKV_W + GQA_W]),
        yn_ref[...] * _silu(g[:, RWKV_W + GQA_W:]),
    )
    mixed = jnp.concatenate([p.astype(BF16) for p in parts], axis=1)
    y = _dot(mixed, w_ref[...])
    xn = alpha * x_ref[...] + mod_ref[2:3, :] * y
    mean = jnp.mean(xn, axis=-1, keepdims=True)
    xc = xn - mean
    var = jnp.mean(xc * xc, axis=-1, keepdims=True)
    o_ref[...] = xc * lax.rsqrt(var + LN_EPS) * lng_ref[...] + lnb_ref[...]


def _output(xc, yf, yb, bonus, ya, yn, gates, modsel, w_out, gn_g, gn_b, ln_g, ln_b, ones_heads, alpha, n_ctx_tiles,
            skip_tiles):
    b, t, d = xc.shape
    nt = t // TOKEN_TILE - skip_tiles
    tok = lambda width: pl.BlockSpec((None, TOKEN_TILE, width), lambda bi, i: (bi, i + skip_tiles, 0))
    full = lambda shape: pl.BlockSpec(shape, lambda bi, i: (0,) * len(shape))
    return pl.pallas_call(
        functools.partial(_out_kernel, alpha=alpha),
        grid=(b, nt),
        in_specs=[
            tok(d), tok(RWKV_W), tok(RWKV_W), tok(RWKV_W), tok(GQA_W), tok(NA_W), tok(RWKV_W + GQA_W + NA_W),
            pl.BlockSpec((None, None, 3, d), lambda bi, i: (bi, jnp.where(i + skip_tiles < n_ctx_tiles, 0, 1), 0, 0)),
            full(w_out.shape), full((1, RWKV_W)), full((1, RWKV_W)), full((1, d)), full((1, d)), full((2 * RWKV_W, RWKV_W)),
        ],
        out_specs=pl.BlockSpec((None, TOKEN_TILE, d), lambda bi, i: (bi, i, 0)),
        out_shape=jax.ShapeDtypeStruct((b, nt * TOKEN_TILE, d), F32),
        compiler_params=_cparams("parallel", "parallel"),
        name="out",
    )(xc, yf, yb, bonus, ya, yn, gates, modsel, w_out, gn_g, gn_b, ln_g, ln_b, ones_heads)


def _rope_tables(n_ctx, n_lat):
    t = jnp.arange(n_lat, dtype=jnp.int32)
    row = (t // GRID_W).astype(F32)
    col = (t % GRID_W).astype(F32)
    n_freq = HEAD_DIM // 4
    inv_freq = ROPE_THETA ** (-jnp.arange(n_freq, dtype=F32) / n_freq)
    ang_r = row[:, None] * inv_freq
    ang_c = col[:, None] * inv_freq
    ang = jnp.concatenate([ang_r, ang_r, ang_c, ang_c], axis=-1)
    cos = jnp.concatenate([jnp.ones((n_ctx, HEAD_DIM), F32), jnp.cos(ang)], axis=0)
    sin = jnp.concatenate([jnp.zeros((n_ctx, HEAD_DIM), F32), jnp.sin(ang)], axis=0)
    first = (np.arange(HEAD_DIM) % (HEAD_DIM // 2)) < HEAD_DIM // 4
    sin_a = jnp.where(first, -sin, 0.0)
    sin_b = jnp.where(first, 0.0, sin)
    pair = lambda z: jnp.concatenate([z, z], axis=-1)
    return pair(cos), pair(sin_a), pair(sin_b)


def _block_ones(n_heads):
    blk = np.kron(np.eye(n_heads, dtype=np.float32), np.ones((HEAD_DIM, HEAD_DIM), np.float32))
    return jnp.asarray(blk)


def kernel(x, c, ctx, c_ctx, w_mod, b_mod, w_in, w_out, rwkv_conv, decay_w0, decay_w2, iclr_a0, iclr_a2, rwkv_k_k, rwkv_k_a, rwkv_r_k, rwkv_gn_g, rwkv_gn_b, gqa_q_norm, gqa_k_norm, na_rpb, ln_g, ln_b):
    b, n_lat, d = x.shape
    n_ctx = ctx.shape[1]
    depth = w_mod.shape[0]
    alpha = float((2 * depth) ** 0.25)
    n_ctx_tiles = n_ctx // TOKEN_TILE
    assert n_ctx % TOKEN_TILE == 0 and n_lat % TOKEN_TILE == 0 and n_lat % GRID_W == 0

    n_rows = -(-(b + 1) // SUBLANES) * SUBLANES
    cvec = jnp.zeros((n_rows, d), F32).at[:b].set(c).at[b].set(c_ctx)
    mod = _modulation(cvec, w_mod, b_mod).reshape(depth, n_rows, 3, d)
    mod_ctx = jnp.broadcast_to(mod[:, b][:, None], (depth, b, 3, d))
    modsel = jnp.stack([mod_ctx, mod[:, :b]], axis=2)

    sizes = (RWKV_CONV_W, RWKV_W, GQA_W, GQA_KV_W, GQA_KV_W, GQA_W, NA_W, NA_W, NA_W, NA_W)
    offs = np.concatenate([[0], np.cumsum(sizes)])
    seg = lambda n: w_in[:, :, offs[n]:offs[n + 1]]

    def pair_heads(z, axis):
        shp = z.shape[:axis] + (GQA_KV_HEADS, GQA_Q_HEADS // GQA_KV_HEADS, HEAD_DIM) + z.shape[axis + 1:]
        return jnp.swapaxes(z.reshape(shp), axis, axis + 1).reshape(z.shape)

    w_in_p = jnp.concatenate([seg(0), pair_heads(seg(2), 2), seg(3), seg(4), seg(6), seg(7), seg(8),
                              seg(1), pair_heads(seg(5), 2), seg(9)], axis=2).astype(BF16)
    w_out_p = jnp.concatenate([w_out[:, :RWKV_W], pair_heads(w_out[:, RWKV_W:RWKV_W + GQA_W], 1),
                               w_out[:, RWKV_W + GQA_W:]], axis=1).astype(BF16)

    w = RWKV_W
    zero = jnp.zeros((depth, LOW_RANK, w), F32)
    blocks = (decay_w2[:, 0], decay_w2[:, 1], iclr_a2[:, 0], iclr_a2[:, 1])
    wlr = jnp.concatenate([jnp.concatenate([blk if cc == rr else zero for cc in range(4)], axis=2)
                           for rr, blk in enumerate(blocks)], axis=1)
    wlr_hi = wlr.astype(BF16)
    wlr_lo = (wlr - wlr_hi.astype(F32)).astype(BF16)
    wlr3 = jnp.concatenate([wlr_hi, wlr_hi, wlr_lo], axis=1)
    b0 = jnp.concatenate([decay_w0[:, 0], decay_w0[:, 1], iclr_a0[:, 0], iclr_a0[:, 1]], axis=-1)[:, None, :]

    cos, sin_a, sin_b = _rope_tables(n_ctx, n_lat)
    ones_heads = jnp.tile(_block_ones(RWKV_HEADS), (2, 1)).astype(BF16)
    ones_pair = jnp.tile(_block_ones(2), (2, 1)).astype(BF16)
    pair = lambda z: jnp.concatenate([z, z], axis=-1)[:, None, :]
    qn, kn = pair(gqa_q_norm), pair(gqa_k_norm)

    xc = jnp.concatenate([ctx, x], axis=1)
    t = n_ctx + n_lat
    kv_chunk = next(ch for ch in KV_CHUNKS if t % ch == 0)
    for l in range(depth):
        ru, aq, ak, av, nq4, nk, nv, gates = _projection(xc, modsel[l], w_in_p[l], n_ctx_tiles)
        prep = _rwkv_prep(ru, rwkv_conv[l], wlr3[l], b0[l], rwkv_k_k[l][None], rwkv_k_a[l][None],
                          rwkv_r_k[l].reshape(1, w), ones_heads, n_ctx_tiles)
        yf, yb = _rwkv_scan(prep, n_ctx // SCAN_CHUNK)
        last = l == depth - 1
        qt, kr, vt4 = _qk_prep(aq, ak, av, cos, sin_a, sin_b, qn[l], kn[l], ones_pair, kv_chunk, n_ctx_tiles)
        k4 = kr.reshape(b, t // kv_chunk, kv_chunk, GQA_KV_W)
        ya = _gqa(qt, k4, vt4, 0, n_lat)
        if last:
            ya = jnp.concatenate([jnp.zeros((b, n_ctx, GQA_W), F32), ya], axis=1)
        else:
            ya_c = _gqa(qt, kr[:, :n_ctx].reshape(b, 1, n_ctx, GQA_KV_W), vt4[:, :1, :, :n_ctx], n_lat, n_ctx)
            ya = jnp.concatenate([ya_c, ya], axis=1)
        yn = _natten(nq4, nk, nv, _natten_bias(na_rpb[l]), n_ctx)
        xc = _output(xc, yf, yb, prep[9], ya, yn, gates, modsel[l], w_out_p[l], rwkv_gn_g[l][None], rwkv_gn_b[l][None],
                     ln_g[l][None], ln_b[l][None], ones_heads, alpha, n_ctx_tiles, n_ctx_tiles if last else 0)
    return xc
```

```python
import functools

import numpy as np
import jax
import jax.numpy as jnp
from jax import lax
from jax.experimental import pallas as pl
from jax.experimental.pallas import tpu as pltpu

F32 = jnp.float32
BF16 = jnp.bfloat16
HI = lax.Precision.HIGHEST

HEAD_DIM = 64
RWKV_HEADS = 6
GQA_Q_HEADS = 6
GQA_KV_HEADS = 2
NA_HEADS = 4
RWKV_W = RWKV_HEADS * HEAD_DIM
GQA_W = GQA_Q_HEADS * HEAD_DIM
GQA_KV_W = GQA_KV_HEADS * HEAD_DIM
NA_W = NA_HEADS * HEAD_DIM
LOW_RANK = 32
RWKV_CONV_W = 3 * RWKV_W + 4 * LOW_RANK
GRID_W = 64
NA_WIN_ROWS = 8
NA_WIN_COLS = 16
ROPE_THETA = 10000.0
LN_EPS = 1e-5
RMS_EPS = 1e-6
GN_EPS = 64e-5
KK_EPS = 1e-12
MASK_BIAS = -1e30

LANES = 128
SUBLANES = 8
TOKEN_TILE = 256
SCAN_CHUNK = 64
SCAN_CHUNKS_PER_STEP = 4
KV_CHUNKS = (768, 512, 256)
GQA_Q_TILES = (512, 256)
PV_ROWS = HEAD_DIM + 16
LOG2_E = 1.4426950408889634
VMEM_LIMIT = 56 * 1024 * 1024


def _cparams(*sem):
    return pltpu.CompilerParams(dimension_semantics=sem, vmem_limit_bytes=VMEM_LIMIT)


def _dot(a, b, precision=None):
    return jnp.dot(a, b, preferred_element_type=F32, precision=precision)


def _dot_nt(a, b, precision=None):
    return lax.dot_general(a, b, (((1,), (1,)), ((), ())), preferred_element_type=F32, precision=precision)


def _dot_tn(a, b, precision=None):
    return lax.dot_general(a, b, (((0,), (0,)), ((), ())), preferred_element_type=F32, precision=precision)


def _sigmoid(x):
    return 1.0 / (1.0 + jnp.exp(-x))


def _silu(x):
    return x * _sigmoid(x)


def _mod_kernel(c_ref, w_ref, b_ref, o_ref):
    o_ref[...] = _dot(_silu(c_ref[...]), w_ref[...], HI) + b_ref[...]


def _modulation(cvec, w_mod, b_mod):
    depth, d, d3 = w_mod.shape
    r = cvec.shape[0]
    nj = d3 // d
    return pl.pallas_call(
        _mod_kernel,
        grid=(depth, nj),
        in_specs=[
            pl.BlockSpec((r, d), lambda l, j: (0, 0)),
            pl.BlockSpec((None, d, d), lambda l, j: (l, 0, j)),
            pl.BlockSpec((None, 1, d), lambda l, j: (l, 0, j)),
        ],
        out_specs=pl.BlockSpec((None, r, d), lambda l, j: (l, 0, j)),
        out_shape=jax.ShapeDtypeStruct((depth, r, d3), F32),
        compiler_params=_cparams("parallel", "parallel"),
        name="modulation",
    )(cvec, w_mod, b_mod.reshape(depth, 1, d3))


_PROJ_SLABS = (
    ("ru", RWKV_CONV_W, F32),
    ("aq", GQA_W, F32),
    ("ak", GQA_KV_W, F32),
    ("av", GQA_KV_W, BF16),
    ("nq", NA_W, None),
    ("nk", NA_W, BF16),
    ("nv", NA_W, BF16),
    ("gates", RWKV_W + GQA_W + NA_W, BF16),
)


def _proj_kernel(x_ref, mod_ref, w_ref, ru_ref, aq_ref, ak_ref, av_ref, nq_ref, nk_ref, nv_ref, g_ref):
    shift = mod_ref[0:1, :]
    scale = mod_ref[1:2, :]
    h = (x_ref[...] * (1.0 + scale) + shift).astype(BF16)
    outs = {"ru": ru_ref, "aq": aq_ref, "ak": ak_ref, "av": av_ref, "nk": nk_ref, "nv": nv_ref, "gates": g_ref}
    off = 0
    for name, width, _ in _PROJ_SLABS:
        u = _dot(h, w_ref[:, off:off + width])
        if name == "nq":
            u = u * (HEAD_DIM ** -0.5)
            lane = lax.broadcasted_iota(jnp.int32, (u.shape[0], LANES), 1)
            for hd in range(NA_HEADS):
                pair = u[:, (hd // 2) * LANES:(hd // 2 + 1) * LANES]
                keep = (lane < HEAD_DIM) if hd % 2 == 0 else (lane >= HEAD_DIM)
                nq_ref[:, hd * LANES:(hd + 1) * LANES] = jnp.where(keep, pair, 0.0).astype(BF16)
        else:
            outs[name][...] = u.astype(outs[name].dtype)
        off += width


def _projection(xc, modsel, w_perm, n_ctx_tiles):
    b, t, d = xc.shape
    nt = t // TOKEN_TILE
    n_in = w_perm.shape[1]
    shapes, specs = [], []
    for name, width, dt in _PROJ_SLABS:
        if name == "nq":
            width, dt = NA_HEADS * LANES, BF16
        shapes.append(jax.ShapeDtypeStruct((b, t, width), dt))
        specs.append(pl.BlockSpec((None, TOKEN_TILE, width), lambda bi, i: (bi, i, 0)))
    return pl.pallas_call(
        _proj_kernel,
        grid=(b, nt),
        in_specs=[
            pl.BlockSpec((None, TOKEN_TILE, d), lambda bi, i: (bi, i, 0)),
            pl.BlockSpec((None, None, 3, d), lambda bi, i: (bi, jnp.where(i < n_ctx_tiles, 0, 1), 0, 0)),
            pl.BlockSpec((d, n_in), lambda bi, i: (0, 0)),
        ],
        out_specs=specs,
        out_shape=shapes,
        compiler_params=_cparams("parallel", "parallel"),
        name="proj",
    )(xc, modsel, w_perm)


def _rwkv_prep_kernel(x_ref, xp_ref, xn_ref, cw_ref, wlr_ref, b0_ref, kk_ref, ka_ref, rk_ref, ones_ref,
                      r_o, v_o, kk_o, lw0_o, kd0_o, bd0_o, lw1_o, kd1_o, bd1_o, bonus_o, pad_ref, *, n_ctx_tiles):
    i = pl.program_id(1)
    nt = pl.num_programs(1)
    tm = x_ref.shape[0]
    prev_ok = jnp.logical_and(i != 0, i != n_ctx_tiles)
    next_ok = jnp.logical_and(i != n_ctx_tiles - 1, i != nt - 1)
    pad_ref[0:SUBLANES, :] = jnp.where(prev_ok, xp_ref[...], 0.0)
    pad_ref[SUBLANES:SUBLANES + tm, :] = x_ref[...]
    pad_ref[SUBLANES + tm:2 * SUBLANES + tm, :] = jnp.where(next_ok, xn_ref[...], 0.0)
    x_prev = pad_ref[pl.ds(SUBLANES - 1, tm), :]
    x_next = pad_ref[pl.ds(SUBLANES + 1, tm), :]
    u = cw_ref[0:1, :] * x_prev + cw_ref[1:2, :] * x_ref[...] + cw_ref[2:3, :] * x_next

    w = RWKV_W
    r, k, v = u[:, 0:w], u[:, w:2 * w], u[:, 2 * w:3 * w]
    lr = u[:, 3 * w:3 * w + LANES]
    lane = lax.broadcasted_iota(jnp.int32, lr.shape, 1)
    z = jnp.where(lane < 2 * LOW_RANK, jnp.tanh(lr), lr)
    z_hi, z_lo = _split2(z)
    pre = _dot(jnp.concatenate([z_hi, z_lo, z_hi], axis=1), wlr_ref[...]) + b0_ref[...]

    ones = ones_ref[...]
    kk0 = k * kk_ref[...]
    ss = _head_sums(kk0 * kk0, ones)
    kk = kk0 * lax.rsqrt(jnp.maximum(ss, KK_EPS))
    r_o[...] = r
    v_o[...] = v
    kk_o[...] = kk
    kd_sum = None
    for d, (lw_o, kd_o, bd_o) in enumerate(((lw0_o, kd0_o, bd0_o), (lw1_o, kd1_o, bd1_o))):
        xd = -pre[:, d * w:(d + 1) * w]
        softplus = jnp.maximum(xd, 0.0) + jnp.log(1.0 + jnp.exp(-jnp.abs(xd)))
        lw_o[...] = -jnp.exp(-softplus - 0.5)
        iclr = _sigmoid(pre[:, (2 + d) * w:(3 + d) * w])
        kd = k * (1.0 + (iclr - 1.0) * ka_ref[...])
        kd_o[...] = kd
        bd_o[...] = kk * iclr
        kd_sum = kd if kd_sum is None else kd_sum + kd
    bonus_o[...] = _head_sums(r * kd_sum * rk_ref[...], ones) * v


def _rwkv_prep(ru, conv_w, wlr, b0, k_k, k_a, r_k, ones_heads, n_ctx_tiles):
    b, t, cw = ru.shape
    nt = t // TOKEN_TILE
    hb = TOKEN_TILE // SUBLANES
    w = RWKV_W
    full = lambda shape: pl.BlockSpec(shape, lambda bi, i: (0,) * len(shape))
    out_spec = pl.BlockSpec((None, TOKEN_TILE, w), lambda bi, i: (bi, i, 0))
    return pl.pallas_call(
        functools.partial(_rwkv_prep_kernel, n_ctx_tiles=n_ctx_tiles),
        grid=(b, nt),
        in_specs=[
            pl.BlockSpec((None, TOKEN_TILE, cw), lambda bi, i: (bi, i, 0)),
            pl.BlockSpec((None, SUBLANES, cw), lambda bi, i: (bi, jnp.maximum(i * hb - 1, 0), 0)),
            pl.BlockSpec((None, SUBLANES, cw), lambda bi, i: (bi, jnp.minimum((i + 1) * hb, t // SUBLANES - 1), 0)),
            full((3, cw)), full((3 * LANES, 4 * w)), full((1, 4 * w)),
            full((1, w)), full((1, w)), full((1, w)), full((2 * w, w)),
        ],
        out_specs=[out_spec] * 10,
        out_shape=[jax.ShapeDtypeStruct((b, t, w), F32)] * 10,
        scratch_shapes=[pltpu.VMEM((TOKEN_TILE + 2 * SUBLANES, cw), F32)],
        compiler_params=_cparams("parallel", "parallel"),
        name="rwkv_prep",
    )(ru, ru, ru, conv_w, wlr, b0, k_k, k_a, r_k, ones_heads)


def _split2(x):
    hi = x.astype(BF16)
    return hi, (x - hi.astype(F32)).astype(BF16)


def _head_sums(x, ones2):
    hi, lo = _split2(x)
    return _dot(jnp.concatenate([hi, lo], axis=1), ones2)


def _b(x):
    return x.astype(BF16)


def _bdot(a, b):
    return lax.dot_general(a, b, (((2,), (1,)), ((0,), (0,))), preferred_element_type=F32)


def _bdot_nt(a, b):
    return lax.dot_general(a, b, (((2,), (2,)), ((0,), (0,))), preferred_element_type=F32)


def _bdot_tn(a, b):
    return lax.dot_general(a, b, (((1,), (1,)), ((0,), (0,))), preferred_element_type=F32)


def _chunk_local(r, k, v, kap, bet, lw, consts):
    tri, rev, strict, incl, head_rows, eye2, eye_l = consts
    c = r.shape[1]
    l1 = lw.astype(BF16)
    l2 = (lw - l1.astype(F32)).astype(BF16)
    l3 = (lw - l1.astype(F32) - l2.astype(F32)).astype(BF16)
    lcs = _bdot(tri, jnp.concatenate([l1, l2, l3], axis=2))
    lc = lcs[:, :, 0:LANES] + lcs[:, :, LANES:2 * LANES] + lcs[:, :, 2 * LANES:3 * LANES]
    ltot = jnp.where(rev, lc[:, 0:1, :], lc[:, c - 1:c, :])
    e_in = jnp.exp(lc)
    e_inv = jnp.exp(-lc)
    e_hat = jnp.exp(ltot - lc)

    def stack(x):
        return jnp.where(head_rows, jnp.concatenate([x, x], axis=1), 0.0)

    r_s = stack(r * e_in)
    kap_s = stack(kap * jnp.exp(lc - lw))
    k_s = stack(k * e_inv)
    b_s = stack(bet * e_inv)
    kh_s = stack(k * e_hat)
    bh_s = stack(bet * e_hat)
    v_s = stack(v)

    m = _bdot_nt(_b(jnp.concatenate([kap_s, r_s], axis=1)), _b(jnp.concatenate([b_s, k_s], axis=1)))
    c2 = 2 * c
    a = jnp.where(strict, m[:, 0:c2, 0:c2], 0.0)
    bm = jnp.where(strict, m[:, 0:c2, c2:2 * c2], 0.0)
    mb = jnp.where(incl, m[:, c2:2 * c2, 0:c2], 0.0)
    mk = jnp.where(incl, m[:, c2:2 * c2, c2:2 * c2], 0.0)

    t_inv = eye2 - a
    pw = a
    for _ in range(int(np.log2(c)) - 1):
        pw_b = _b(_bdot(_b(pw), _b(pw)))
        pw = pw_b
        t_inv = t_inv + _bdot(_b(t_inv), pw_b)

    v_b = _b(v_s)
    bv = _bdot(_b(bm), v_b)
    x_b = _b(_bdot(_b(t_inv), _b(jnp.concatenate([kap_s, bv], axis=2))))
    z = _bdot(_b(mb), x_b)
    rp = r_s - z[:, :, 0:LANES]
    y0 = _bdot(_b(mk), v_b) - z[:, :, LANES:2 * LANES]
    bhx = _bdot_tn(_b(bh_s), x_b)
    g = jnp.where(eye_l, jnp.exp(ltot), 0.0) - bhx[:, :, 0:LANES]
    hadd = _bdot_tn(_b(kh_s), v_b) - bhx[:, :, LANES:2 * LANES]
    return rp, y0, g, hadd


def _rwkv_scan_kernel(rf, vf, kkf, lwf, kdf, bdf, rb, vb, kkb, lwb, kdb, bdb, yf_ref, yb_ref, h_ref):
    i = pl.program_id(1)

    @pl.when(i == 0)
    def _():
        h_ref[...] = jnp.zeros_like(h_ref)

    c = SCAN_CHUNK
    c2 = 2 * c
    n_slots = rf.shape[0] // c
    n_pairs = RWKV_W // LANES
    n_dir = 2 * n_pairs
    n = n_slots * n_dir
    iota = lambda shape, dim: lax.broadcasted_iota(jnp.int32, shape, dim)
    is_rev = lambda shape: lax.rem(iota(shape, 0), n_dir) >= n_pairs
    rev = is_rev((n, 1, 1))
    sign = lambda shape: jnp.where(is_rev(shape), 1, -1)
    tri = jnp.where((iota((n, c, c), 2) - iota((n, c, c), 1)) * sign((n, c, c)) >= 0, 1.0, 0.0).astype(BF16)
    row2, col2 = iota((n, c2, c2), 1), iota((n, c2, c2), 2)
    same = (row2 < c) == (col2 < c)
    rt = jnp.where(row2 < c, row2, row2 - c)
    ct = jnp.where(col2 < c, col2, col2 - c)
    before = (ct - rt) * sign((n, c2, c2))
    strict = jnp.logical_and(same, before > 0)
    incl = jnp.logical_and(same, before >= 0)
    head_rows = (iota((n, c2, LANES), 1) < c) == (iota((n, c2, LANES), 2) < HEAD_DIM)
    eye2 = jnp.where(row2 == col2, 1.0, 0.0)
    eye_l = iota((n, LANES, LANES), 1) == iota((n, LANES, LANES), 2)
    consts = (tri, rev, strict, incl, head_rows, eye2, eye_l)

    def rows(s, reverse):
        s = n_slots - 1 - s if reverse else s
        return slice(s * c, (s + 1) * c)

    def chains(fref, bref):
        return jnp.stack([ref[rows(s, ref is bref), p * LANES:(p + 1) * LANES]
                          for s in range(n_slots) for ref in (fref, bref) for p in range(n_pairs)], axis=0)

    rp, y0, g, hadd = _chunk_local(chains(rf, rb), chains(kdf, kdb), chains(vf, vb), chains(kkf, kkb),
                                   chains(bdf, bdb), chains(lwf, lwb), consts)
    h = h_ref[...]
    for s in range(n_slots):
        cs = slice(s * n_dir, (s + 1) * n_dir)
        h_b = _b(h)
        ys = _bdot(_b(rp[cs]), h_b) + y0[cs]
        h = _bdot(_b(g[cs]), h_b) + hadd[cs]
        y = ys[:, 0:c, :] + ys[:, c:c2, :]
        for p in range(n_pairs):
            yf_ref[rows(s, False), p * LANES:(p + 1) * LANES] = y[p]
            yb_ref[rows(s, True), p * LANES:(p + 1) * LANES] = y[n_pairs + p]
    h_ref[...] = h


def _rwkv_scan(prep, n_ctx):
    r, v, kk, lw0, kd0, bd0, lw1, kd1, bd1, _ = prep
    b, t, w = r.shape
    blk = SCAN_CHUNK * SCAN_CHUNKS_PER_STEP
    assert n_ctx % blk == 0 and t % blk == 0
    nb, ncb = t // blk, n_ctx // blk

    def bwd_block(i):
        return jnp.where(i < ncb, ncb - 1 - i, nb - 1 + ncb - i)

    fspec = pl.BlockSpec((None, blk, w), lambda bi, i: (bi, i, 0))
    bspec = pl.BlockSpec((None, blk, w), lambda bi, i: (bi, bwd_block(i), 0))
    return pl.pallas_call(
        _rwkv_scan_kernel,
        grid=(b, nb),
        in_specs=[fspec] * 6 + [bspec] * 6,
        out_specs=[fspec, bspec],
        out_shape=[jax.ShapeDtypeStruct((b, t, w), F32)] * 2,
        scratch_shapes=[pltpu.VMEM((2 * (w // LANES), LANES, LANES), F32)],
        compiler_params=_cparams("parallel", "arbitrary"),
        name="rwkv_scan",
    )(r, v, kk, lw0, kd0, bd0, r, v, kk, lw1, kd1, bd1)


def _qk_prep_kernel(aq_ref, ak_ref, av_ref, cos_ref, sa_ref, sb_ref, qn_ref, kn_ref, ones_ref, qt_o, k_o, vt_o):
    cos, sin_a, sin_b = cos_ref[...], sa_ref[...], sb_ref[...]
    ones = ones_ref[...]
    tm = cos.shape[0]
    row = lax.broadcasted_iota(jnp.int32, (LANES, tm), 0)

    def norm_rope(x, g):
        ms = _head_sums(x * x, ones) * (1.0 / HEAD_DIM)
        xn = x * lax.rsqrt(ms + RMS_EPS) * g
        return (xn * cos + pltpu.roll(xn, LANES - HEAD_DIM // 4, 1) * sin_a
                + pltpu.roll(xn, HEAD_DIM // 4, 1) * sin_b)

    for j in range(GQA_W // LANES):
        q = norm_rope(aq_ref[:, j * LANES:(j + 1) * LANES], qn_ref[...]) * (HEAD_DIM ** -0.5 * LOG2_E)
        qt = q.T
        qt_o[(2 * j) * LANES:(2 * j + 1) * LANES, :] = jnp.where(row < HEAD_DIM, qt, 0.0).astype(BF16)
        qt_o[(2 * j + 1) * LANES:(2 * j + 2) * LANES, :] = jnp.where(row >= HEAD_DIM, qt, 0.0).astype(BF16)
    k_o[...] = norm_rope(ak_ref[...], kn_ref[...]).astype(BF16)
    vt = av_ref[...].astype(F32).T
    one = jnp.ones((PV_ROWS - HEAD_DIM, tm), F32)
    vt_o[...] = jnp.concatenate([vt[0:HEAD_DIM], one, vt[HEAD_DIM:2 * HEAD_DIM], one], axis=0).astype(BF16)


def _qk_prep(aq, ak, av, cos, sin_a, sin_b, qn, kn, ones_pair, kv_chunk, n_ctx_tiles):
    b, t, _ = aq.shape
    nt = t // TOKEN_TILE
    per = kv_chunk // TOKEN_TILE
    tok = lambda width: pl.BlockSpec((None, TOKEN_TILE, width), lambda bi, i: (bi, i, 0))
    tab = pl.BlockSpec((TOKEN_TILE, LANES), lambda bi, i: (i, 0))
    full = lambda shape: pl.BlockSpec(shape, lambda bi, i: (0,) * len(shape))
    return pl.pallas_call(
        _qk_prep_kernel,
        grid=(b, nt),
        in_specs=[tok(GQA_W), tok(GQA_KV_W), tok(GQA_KV_W), tab, tab, tab,
                  full((1, LANES)), full((1, LANES)), full((2 * LANES, LANES))],
        out_specs=[
            pl.BlockSpec((None, 2 * GQA_W, TOKEN_TILE),
                         lambda bi, i: (bi, 0, jnp.where(i < n_ctx_tiles, i + nt - n_ctx_tiles, i - n_ctx_tiles))),
            tok(GQA_KV_W),
            pl.BlockSpec((None, None, GQA_KV_HEADS * PV_ROWS, TOKEN_TILE), lambda bi, i: (bi, i // per, 0, i % per)),
        ],
        out_shape=[
            jax.ShapeDtypeStruct((b, 2 * GQA_W, t), BF16),
            jax.ShapeDtypeStruct((b, t, GQA_KV_W), BF16),
            jax.ShapeDtypeStruct((b, t // kv_chunk, GQA_KV_HEADS * PV_ROWS, kv_chunk), BF16),
        ],
        compiler_params=_cparams("parallel", "parallel"),
        name="qk_prep",
    )(aq, ak, av, cos, sin_a, sin_b, qn, kn, ones_pair)


def _gqa_kernel(qt_ref, k_ref, vt_ref, o_ref, acc_ref):
    n_chunks = k_ref.shape[0]
    tq = qt_ref.shape[1]
    nj = GQA_W // LANES
    acc_ref[...] = jnp.zeros_like(acc_ref)

    def body(c, ms):
        kc = k_ref[c]
        new_ms = []
        for g in range(GQA_KV_HEADS):
            qt = jnp.concatenate([qt_ref[(2 * j + g) * LANES:(2 * j + g + 1) * LANES, :] for j in range(nj)], axis=1)
            s = _dot(kc, qt)
            vt = vt_ref[c, g * PV_ROWS:(g + 1) * PV_ROWS, :]
            m_new = jnp.maximum(ms[g], jnp.max(s, axis=0, keepdims=True))
            alpha = jnp.exp2(ms[g] - m_new)
            p = jnp.exp2(s - m_new).astype(BF16)
            for j in range(nj):
                sl = slice(j * tq, (j + 1) * tq)
                acc_ref[g, j] = alpha[:, sl] * acc_ref[g, j] + _dot(vt, p[:, sl])
            new_ms.append(m_new)
        return tuple(new_ms)

    m0 = jnp.full((1, nj * tq), -jnp.inf, F32)
    lax.fori_loop(0, n_chunks, body, (m0,) * GQA_KV_HEADS)
    for j in range(nj):
        outs = []
        for g in range(GQA_KV_HEADS):
            acc = acc_ref[g, j]
            outs.append(acc[0:HEAD_DIM] / acc[HEAD_DIM:HEAD_DIM + 1])
        o_ref[:, j * LANES:(j + 1) * LANES] = jnp.concatenate(outs, axis=0).T


def _gqa(qt, k4, vt4, q_start, n_q):
    b = qt.shape[0]
    _, n_chunks, kv_chunk, _ = k4.shape
    tq = next(w for w in GQA_Q_TILES if n_q % w == 0 and q_start % w == 0)
    q0 = q_start // tq
    return pl.pallas_call(
        _gqa_kernel,
        grid=(b, n_q // tq),
        in_specs=[
            pl.BlockSpec((None, 2 * GQA_W, tq), lambda bi, i: (bi, 0, i + q0)),
            pl.BlockSpec((None, n_chunks, kv_chunk, GQA_KV_W), lambda bi, i: (bi, 0, 0, 0)),
            pl.BlockSpec((None, n_chunks, GQA_KV_HEADS * PV_ROWS, kv_chunk), lambda bi, i: (bi, 0, 0, 0)),
        ],
        out_specs=pl.BlockSpec((None, tq, GQA_W), lambda bi, i: (bi, i, 0)),
        out_shape=jax.ShapeDtypeStruct((b, n_q, GQA_W), F32),
        scratch_shapes=[pltpu.VMEM((GQA_KV_HEADS, GQA_W // LANES, PV_ROWS, tq), F32)],
        compiler_params=_cparams("parallel", "parallel"),
        name="gqa",
    )(qt, k4, vt4)


def _natten_kernel(q_ref, k_ref, v_ref, bias_ref, o_ref, *, n_ctx, rows):
    i = pl.program_id(1)
    nq = GRID_W
    rows_per_step = q_ref.shape[0] // nq
    win = NA_WIN_ROWS * GRID_W
    is_ctx = i * rows_per_step < n_ctx // nq
    n_pairs = NA_W // LANES
    qs, kws, vws, kcs, vcs, bs = [], [], [], [], [], []
    for rr in range(rows_per_step):
        r = i * rows_per_step + rr - n_ctx // nq
        r_start = jnp.clip(r - NA_WIN_ROWS // 2, 0, rows - NA_WIN_ROWS)
        off = jnp.where(is_ctx, NA_WIN_ROWS, r_start - r + NA_WIN_ROWS - 1)
        koff = pl.multiple_of(n_ctx + r_start * GRID_W, GRID_W)
        for pair in range(n_pairs):
            sl = slice(pair * LANES, (pair + 1) * LANES)
            kw, vw = k_ref[pl.ds(koff, win), sl], v_ref[pl.ds(koff, win), sl]
            kc, vc = k_ref[0:n_ctx, sl], v_ref[0:n_ctx, sl]
            for half in range(2):
                hd = 2 * pair + half
                qs.append(q_ref[rr * nq:(rr + 1) * nq, hd * LANES:(hd + 1) * LANES])
                bs.append(bias_ref[hd, off])
                kws.append(kw)
                vws.append(vw)
                kcs.append(kc)
                vcs.append(vc)
    q = jnp.stack(qs, axis=0)
    s_w = _bdot_nt(q, jnp.stack(kws, axis=0)) + jnp.stack(bs, axis=0)
    s_c = _bdot_nt(q, jnp.stack(kcs, axis=0))
    m = jnp.maximum(jnp.max(s_w, axis=-1, keepdims=True), jnp.max(s_c, axis=-1, keepdims=True))
    p_w = jnp.exp(s_w - m)
    p_c = jnp.exp(s_c - m)
    l = jnp.sum(p_w, axis=-1, keepdims=True) + jnp.sum(p_c, axis=-1, keepdims=True)
    o = (_bdot(p_w.astype(BF16), jnp.stack(vws, axis=0)) + _bdot(p_c.astype(BF16), jnp.stack(vcs, axis=0))) / l
    lane = lax.broadcasted_iota(jnp.int32, (nq, LANES), 1)
    for rr in range(rows_per_step):
        for pair in range(n_pairs):
            e = (rr * n_pairs + pair) * 2
            o_ref[rr * nq:(rr + 1) * nq, pair * LANES:(pair + 1) * LANES] = jnp.where(lane < HEAD_DIM, o[e], o[e + 1])


def _natten(nq4, nk, nv, bias, n_ctx):
    b, t, _ = nk.shape
    rows = (t - n_ctx) // GRID_W
    assert rows >= NA_WIN_ROWS and n_ctx % TOKEN_TILE == 0 and TOKEN_TILE % GRID_W == 0
    kv = pl.BlockSpec((None, t, NA_W), lambda bi, i: (bi, 0, 0))
    return pl.pallas_call(
        functools.partial(_natten_kernel, n_ctx=n_ctx, rows=rows),
        grid=(b, t // TOKEN_TILE),
        in_specs=[
            pl.BlockSpec((None, TOKEN_TILE, NA_HEADS * LANES), lambda bi, i: (bi, i, 0)),
            kv, kv,
            pl.BlockSpec(bias.shape, lambda bi, i: (0, 0, 0, 0)),
        ],
        out_specs=pl.BlockSpec((None, TOKEN_TILE, NA_W), lambda bi, i: (bi, i, 0)),
        out_shape=jax.ShapeDtypeStruct((b, t, NA_W), F32),
        compiler_params=_cparams("parallel", "parallel"),
        name="natten",
    )(nq4, nk, nv, bias)


def _natten_bias(rpb):
    cols = np.arange(GRID_W)
    c_start = np.clip(cols - NA_WIN_COLS // 2, 0, GRID_W - NA_WIN_COLS)
    key = np.arange(GRID_W)
    valid = (key[None, :] >= c_start[:, None]) & (key[None, :] < c_start[:, None] + NA_WIN_COLS)
    dc = key[None, :] - cols[:, None] + NA_WIN_COLS - 1
    place = (valid[:, :, None] & (dc[:, :, None] == np.arange(2 * NA_WIN_COLS - 1))).astype(np.float32)
    mask = np.where(valid, 0.0, MASK_BIAS).astype(np.float32)
    rows = jnp.stack([rpb[:, o:o + NA_WIN_ROWS, :] for o in range(NA_WIN_ROWS)], axis=1)
    dense = jnp.einsum("hopd,cxd->hocpx", rows, jnp.asarray(place), precision=HI) + mask[None, None, :, None, :]
    dense = dense.reshape(NA_HEADS, NA_WIN_ROWS, GRID_W, NA_WIN_ROWS * GRID_W)
    return jnp.concatenate([dense, jnp.full_like(dense[:, :1], MASK_BIAS)], axis=1).astype(F32)


def _out_kernel(x_ref, yf_ref, yb_ref, bonus_ref, ya_ref, yn_ref, g_ref, mod_ref, w_ref,
                gng_ref, gnb_ref, lng_ref, lnb_ref, ones_ref, o_ref, *, alpha):
    ones = ones_ref[...]
    ys = yf_ref[...] + yb_ref[...]
    mu = _head_sums(ys, ones) * (1.0 / HEAD_DIM)
    dv = ys - mu
    var = _head_sums(dv * dv, ones) * (1.0 / HEAD_DIM)
    yr = dv * lax.rsqrt(var + GN_EPS) * gng_ref[...] + gnb_ref[...] + bonus_ref[...]
    g = g_ref[...].astype(F32)
    parts = (
        yr * _silu(g[:, 0:RWKV_W]),
        ya_ref[...] * _silu(g[:, RWKV_W:RWKV_W + GQA_W]),
        yn_ref[...] * _silu(g[:, RWKV_W + GQA_W:]),
    )
    mixed = jnp.concatenate([p.astype(BF16) for p in parts], axis=1)
    y = _dot(mixed, w_ref[...])
    xn = alpha * x_ref[...] + mod_ref[2:3, :] * y
    mean = jnp.mean(xn, axis=-1, keepdims=True)
    xc = xn - mean
    var = jnp.mean(xc * xc, axis=-1, keepdims=True)
    o_ref[...] = xc * lax.rsqrt(var + LN_EPS) * lng_ref[...] + lnb_ref[...]


def _output(xc, yf, yb, bonus, ya, yn, gates, modsel, w_out, gn_g, gn_b, ln_g, ln_b, ones_heads, alpha, n_ctx_tiles,
            skip_tiles):
    b, t, d = xc.shape
    nt = t // TOKEN_TILE - skip_tiles
    tok = lambda width: pl.BlockSpec((None, TOKEN_TILE, width), lambda bi, i: (bi, i + skip_tiles, 0))
    full = lambda shape: pl.BlockSpec(shape, lambda bi, i: (0,) * len(shape))
    return pl.pallas_call(
        functools.partial(_out_kernel, alpha=alpha),
        grid=(b, nt),
        in_specs=[
            tok(d), tok(RWKV_W), tok(RWKV_W), tok(RWKV_W), tok(GQA_W), tok(NA_W), tok(RWKV_W + GQA_W + NA_W),
            pl.BlockSpec((None, None, 3, d), lambda bi, i: (bi, jnp.where(i + skip_tiles < n_ctx_tiles, 0, 1), 0, 0)),
            full(w_out.shape), full((1, RWKV_W)), full((1, RWKV_W)), full((1, d)), full((1, d)), full((2 * RWKV_W, RWKV_W)),
        ],
        out_specs=pl.BlockSpec((None, TOKEN_TILE, d), lambda bi, i: (bi, i, 0)),
        out_shape=jax.ShapeDtypeStruct((b, nt * TOKEN_TILE, d), F32),
        compiler_params=_cparams("parallel", "parallel"),
        name="out",
    )(xc, yf, yb, bonus, ya, yn, gates, modsel, w_out, gn_g, gn_b, ln_g, ln_b, ones_heads)


def _rope_tables(n_ctx, n_lat):
    t = jnp.arange(n_lat, dtype=jnp.int32)
    row = (t // GRID_W).astype(F32)
    col = (t % GRID_W).astype(F32)
    n_freq = HEAD_DIM // 4
    inv_freq = ROPE_THETA ** (-jnp.arange(n_freq, dtype=F32) / n_freq)
    ang_r = row[:, None] * inv_freq
    ang_c = col[:, None] * inv_freq
    ang = jnp.concatenate([ang_r, ang_r, ang_c, ang_c], axis=-1)
    cos = jnp.concatenate([jnp.ones((n_ctx, HEAD_DIM), F32), jnp.cos(ang)], axis=0)
    sin = jnp.concatenate([jnp.zeros((n_ctx, HEAD_DIM), F32), jnp.sin(ang)], axis=0)
    first = (np.arange(HEAD_DIM) % (HEAD_DIM // 2)) < HEAD_DIM // 4
    sin_a = jnp.where(first, -sin, 0.0)
    sin_b = jnp.where(first, 0.0, sin)
    pair = lambda z: jnp.concatenate([z, z], axis=-1)
    return pair(cos), pair(sin_a), pair(sin_b)


def _block_ones(n_heads):
    blk = np.kron(np.eye(n_heads, dtype=np.float32), np.ones((HEAD_DIM, HEAD_DIM), np.float32))
    return jnp.asarray(blk)


def kernel(x, c, ctx, c_ctx, w_mod, b_mod, w_in, w_out, rwkv_conv, decay_w0, decay_w2, iclr_a0, iclr_a2, rwkv_k_k, rwkv_k_a, rwkv_r_k, rwkv_gn_g, rwkv_gn_b, gqa_q_norm, gqa_k_norm, na_rpb, ln_g, ln_b):
    b, n_lat, d = x.shape
    n_ctx = ctx.shape[1]
    depth = w_mod.shape[0]
    alpha = float((2 * depth) ** 0.25)
    n_ctx_tiles = n_ctx // TOKEN_TILE
    assert n_ctx % TOKEN_TILE == 0 and n_lat % TOKEN_TILE == 0 and n_lat % GRID_W == 0

    n_rows = -(-(b + 1) // SUBLANES) * SUBLANES
    cvec = jnp.zeros((n_rows, d), F32).at[:b].set(c).at[b].set(c_ctx)
    mod = _modulation(cvec, w_mod, b_mod).reshape(depth, n_rows, 3, d)
    mod_ctx = jnp.broadcast_to(mod[:, b][:, None], (depth, b, 3, d))
    modsel = jnp.stack([mod_ctx, mod[:, :b]], axis=2)

    sizes = (RWKV_CONV_W, RWKV_W, GQA_W, GQA_KV_W, GQA_KV_W, GQA_W, NA_W, NA_W, NA_W, NA_W)
    offs = np.concatenate([[0], np.cumsum(sizes)])
    seg = lambda n: w_in[:, :, offs[n]:offs[n + 1]]

    def pair_heads(z, axis):
        shp = z.shape[:axis] + (GQA_KV_HEADS, GQA_Q_HEADS // GQA_KV_HEADS, HEAD_DIM) + z.shape[axis + 1:]
        return jnp.swapaxes(z.reshape(shp), axis, axis + 1).reshape(z.shape)

    w_in_p = jnp.concatenate([seg(0), pair_heads(seg(2), 2), seg(3), seg(4), seg(6), seg(7), seg(8),
                              seg(1), pair_heads(seg(5), 2), seg(9)], axis=2).astype(BF16)
    w_out_p = jnp.concatenate([w_out[:, :RWKV_W], pair_heads(w_out[:, RWKV_W:RWKV_W + GQA_W], 1),
                               w_out[:, RWKV_W + GQA_W:]], axis=1).astype(BF16)

    w = RWKV_W
    zero = jnp.zeros((depth, LOW_RANK, w), F32)
    blocks = (decay_w2[:, 0], decay_w2[:, 1], iclr_a2[:, 0], iclr_a2[:, 1])
    wlr = jnp.concatenate([jnp.concatenate([blk if cc == rr else zero for cc in range(4)], axis=2)
                           for rr, blk in enumerate(blocks)], axis=1)
    wlr_hi = wlr.astype(BF16)
    wlr_lo = (wlr - wlr_hi.astype(F32)).astype(BF16)
    wlr3 = jnp.concatenate([wlr_hi, wlr_hi, wlr_lo], axis=1)
    b0 = jnp.concatenate([decay_w0[:, 0], decay_w0[:, 1], iclr_a0[:, 0], iclr_a0[:, 1]], axis=-1)[:, None, :]

    cos, sin_a, sin_b = _rope_tables(n_ctx, n_lat)
    ones_heads = jnp.tile(_block_ones(RWKV_HEADS), (2, 1)).astype(BF16)
    ones_pair = jnp.tile(_block_ones(2), (2, 1)).astype(BF16)
    pair = lambda z: jnp.concatenate([z, z], axis=-1)[:, None, :]
    qn, kn = pair(gqa_q_norm), pair(gqa_k_norm)

    xc = jnp.concatenate([ctx, x], axis=1)
    t = n_ctx + n_lat
    kv_chunk = next(ch for ch in KV_CHUNKS if t % ch == 0)
    for l in range(depth):
        ru, aq, ak, av, nq4, nk, nv, gates = _projection(xc, modsel[l], w_in_p[l], n_ctx_tiles)
        prep = _rwkv_prep(ru, rwkv_conv[l], wlr3[l], b0[l], rwkv_k_k[l][None], rwkv_k_a[l][None],
                          rwkv_r_k[l].reshape(1, w), ones_heads, n_ctx_tiles)
        yf, yb = _rwkv_scan(prep, n_ctx)
        last = l == depth - 1
        qt, kr, vt4 = _qk_prep(aq, ak, av, cos, sin_a, sin_b, qn[l], kn[l], ones_pair, kv_chunk, n_ctx_tiles)
        k4 = kr.reshape(b, t // kv_chunk, kv_chunk, GQA_KV_W)
        ya = _gqa(qt, k4, vt4, 0, n_lat)
        if last:
            ya = jnp.concatenate([jnp.zeros((b, n_ctx, GQA_W), F32), ya], axis=1)
        else:
            ya_c = _gqa(qt, kr[:, :n_ctx].reshape(b, 1, n_ctx, GQA_KV_W), vt4[:, :1, :, :n_ctx], n_lat, n_ctx)
            ya = jnp.concatenate([ya_c, ya], axis=1)
        yn = _natten(nq4, nk, nv, _natten_bias(na_rpb[l]), n_ctx)
        xc = _output(xc, yf, yb, prep[9], ya, yn, gates, modsel[l], w_out_p[l], rwkv_gn_g[l][None], rwkv_gn_b[l][None],
                     ln_g[l][None], ln_b[l][None], ones_heads, alpha, n_ctx_tiles, n_ctx_tiles if last else 0)
    return xc
```

```python
import functools

import numpy as np
import jax
import jax.numpy as jnp
from jax import lax
from jax.experimental import pallas as pl
from jax.experimental.pallas import tpu as pltpu

F32 = jnp.float32
BF16 = jnp.bfloat16
HI = lax.Precision.HIGHEST

HEAD_DIM = 64
RWKV_HEADS = 6
GQA_Q_HEADS = 6
GQA_KV_HEADS = 2
NA_HEADS = 4
RWKV_W = RWKV_HEADS * HEAD_DIM
GQA_W = GQA_Q_HEADS * HEAD_DIM
GQA_KV_W = GQA_KV_HEADS * HEAD_DIM
NA_W = NA_HEADS * HEAD_DIM
LOW_RANK = 32
RWKV_CONV_W = 3 * RWKV_W + 4 * LOW_RANK
GRID_W = 64
NA_WIN_ROWS = 8
NA_WIN_COLS = 16
ROPE_THETA = 10000.0
LN_EPS = 1e-5
RMS_EPS = 1e-6
GN_EPS = 64e-5
KK_EPS = 1e-12
MASK_BIAS = -1e30

LANES = 128
SUBLANES = 8
TOKEN_TILE = 256
SCAN_CHUNK = 64
SCAN_CHUNKS_PER_STEP = 4
KV_CHUNKS = (768, 512, 256)
GQA_Q_TILES = (512, 256)
PV_ROWS = HEAD_DIM + 16
FAST_PASS_MIN_DENOM = 2.0 ** -64
LOG2_E = 1.4426950408889634
VMEM_LIMIT = 56 * 1024 * 1024


def _cparams(*sem):
    return pltpu.CompilerParams(dimension_semantics=sem, vmem_limit_bytes=VMEM_LIMIT)


def _dot(a, b, precision=None):
    return jnp.dot(a, b, preferred_element_type=F32, precision=precision)


def _dot_nt(a, b, precision=None):
    return lax.dot_general(a, b, (((1,), (1,)), ((), ())), preferred_element_type=F32, precision=precision)


def _dot_tn(a, b, precision=None):
    return lax.dot_general(a, b, (((0,), (0,)), ((), ())), preferred_element_type=F32, precision=precision)


def _sigmoid(x):
    return 1.0 / (1.0 + jnp.exp(-x))


def _silu(x):
    return x * _sigmoid(x)


def _mod_kernel(c_ref, w_ref, b_ref, o_ref):
    o_ref[...] = _dot(_silu(c_ref[...]), w_ref[...], HI) + b_ref[...]


def _modulation(cvec, w_mod, b_mod):
    depth, d, d3 = w_mod.shape
    r = cvec.shape[0]
    nj = d3 // d
    return pl.pallas_call(
        _mod_kernel,
        grid=(depth, nj),
        in_specs=[
            pl.BlockSpec((r, d), lambda l, j: (0, 0)),
            pl.BlockSpec((None, d, d), lambda l, j: (l, 0, j)),
            pl.BlockSpec((None, 1, d), lambda l, j: (l, 0, j)),
        ],
        out_specs=pl.BlockSpec((None, r, d), lambda l, j: (l, 0, j)),
        out_shape=jax.ShapeDtypeStruct((depth, r, d3), F32),
        compiler_params=_cparams("parallel", "parallel"),
        name="modulation",
    )(cvec, w_mod, b_mod.reshape(depth, 1, d3))


_PROJ_SLABS = (
    ("ru", RWKV_CONV_W, F32),
    ("aq", GQA_W, F32),
    ("ak", GQA_KV_W, F32),
    ("av", GQA_KV_W, BF16),
    ("nq", NA_W, None),
    ("nk", NA_W, BF16),
    ("nv", NA_W, BF16),
    ("gates", RWKV_W + GQA_W + NA_W, BF16),
)


def _proj_kernel(x_ref, mod_ref, w_ref, ru_ref, aq_ref, ak_ref, av_ref, nq_ref, nk_ref, nv_ref, g_ref):
    shift = mod_ref[0:1, :]
    scale = mod_ref[1:2, :]
    h = (x_ref[...] * (1.0 + scale) + shift).astype(BF16)
    outs = {"ru": ru_ref, "aq": aq_ref, "ak": ak_ref, "av": av_ref, "nk": nk_ref, "nv": nv_ref, "gates": g_ref}
    off = 0
    for name, width, _ in _PROJ_SLABS:
        u = _dot(h, w_ref[:, off:off + width])
        if name == "nq":
            u = u * (HEAD_DIM ** -0.5)
            lane = lax.broadcasted_iota(jnp.int32, (u.shape[0], LANES), 1)
            for hd in range(NA_HEADS):
                pair = u[:, (hd // 2) * LANES:(hd // 2 + 1) * LANES]
                keep = (lane < HEAD_DIM) if hd % 2 == 0 else (lane >= HEAD_DIM)
                nq_ref[:, hd * LANES:(hd + 1) * LANES] = jnp.where(keep, pair, 0.0).astype(BF16)
        else:
            outs[name][...] = u.astype(outs[name].dtype)
        off += width


def _projection(xc, modsel, w_perm, n_ctx_tiles):
    b, t, d = xc.shape
    nt = t // TOKEN_TILE
    n_in = w_perm.shape[1]
    shapes, specs = [], []
    for name, width, dt in _PROJ_SLABS:
        if name == "nq":
            width, dt = NA_HEADS * LANES, BF16
        shapes.append(jax.ShapeDtypeStruct((b, t, width), dt))
        specs.append(pl.BlockSpec((None, TOKEN_TILE, width), lambda bi, i: (bi, i, 0)))
    return pl.pallas_call(
        _proj_kernel,
        grid=(b, nt),
        in_specs=[
            pl.BlockSpec((None, TOKEN_TILE, d), lambda bi, i: (bi, i, 0)),
            pl.BlockSpec((None, None, 3, d), lambda bi, i: (bi, jnp.where(i < n_ctx_tiles, 0, 1), 0, 0)),
            pl.BlockSpec((d, n_in), lambda bi, i: (0, 0)),
        ],
        out_specs=specs,
        out_shape=shapes,
        compiler_params=_cparams("parallel", "parallel"),
        name="proj",
    )(xc, modsel, w_perm)


def _rwkv_prep_kernel(x_ref, xp_ref, xn_ref, cw_ref, wlr_ref, b0_ref, kk_ref, ka_ref, rk_ref, ones_ref,
                      r_o, v_o, kk_o, lw0_o, kd0_o, bd0_o, lw1_o, kd1_o, bd1_o, bonus_o, pad_ref, *, n_ctx_tiles):
    i = pl.program_id(1)
    nt = pl.num_programs(1)
    tm = x_ref.shape[0]
    prev_ok = jnp.logical_and(i != 0, i != n_ctx_tiles)
    next_ok = jnp.logical_and(i != n_ctx_tiles - 1, i != nt - 1)
    pad_ref[0:SUBLANES, :] = jnp.where(prev_ok, xp_ref[...], 0.0)
    pad_ref[SUBLANES:SUBLANES + tm, :] = x_ref[...]
    pad_ref[SUBLANES + tm:2 * SUBLANES + tm, :] = jnp.where(next_ok, xn_ref[...], 0.0)
    x_prev = pad_ref[pl.ds(SUBLANES - 1, tm), :]
    x_next = pad_ref[pl.ds(SUBLANES + 1, tm), :]
    u = cw_ref[0:1, :] * x_prev + cw_ref[1:2, :] * x_ref[...] + cw_ref[2:3, :] * x_next

    w = RWKV_W
    r, k, v = u[:, 0:w], u[:, w:2 * w], u[:, 2 * w:3 * w]
    lr = u[:, 3 * w:3 * w + LANES]
    lane = lax.broadcasted_iota(jnp.int32, lr.shape, 1)
    z = jnp.where(lane < 2 * LOW_RANK, jnp.tanh(lr), lr)
    z_hi, z_lo = _split2(z)
    pre = _dot(jnp.concatenate([z_hi, z_lo, z_hi], axis=1), wlr_ref[...]) + b0_ref[...]

    ones = ones_ref[...]
    kk0 = k * kk_ref[...]
    ss = _head_sums(kk0 * kk0, ones)
    kk = kk0 * lax.rsqrt(jnp.maximum(ss, KK_EPS))
    r_o[...] = r
    v_o[...] = v
    kk_o[...] = kk
    kd_sum = None
    for d, (lw_o, kd_o, bd_o) in enumerate(((lw0_o, kd0_o, bd0_o), (lw1_o, kd1_o, bd1_o))):
        xd = -pre[:, d * w:(d + 1) * w]
        softplus = jnp.maximum(xd, 0.0) + jnp.log(1.0 + jnp.exp(-jnp.abs(xd)))
        lw_o[...] = -jnp.exp(-softplus - 0.5)
        iclr = _sigmoid(pre[:, (2 + d) * w:(3 + d) * w])
        kd = k * (1.0 + (iclr - 1.0) * ka_ref[...])
        kd_o[...] = kd
        bd_o[...] = kk * iclr
        kd_sum = kd if kd_sum is None else kd_sum + kd
    bonus_o[...] = _head_sums(r * kd_sum * rk_ref[...], ones) * v


def _rwkv_prep(ru, conv_w, wlr, b0, k_k, k_a, r_k, ones_heads, n_ctx_tiles):
    b, t, cw = ru.shape
    nt = t // TOKEN_TILE
    hb = TOKEN_TILE // SUBLANES
    w = RWKV_W
    full = lambda shape: pl.BlockSpec(shape, lambda bi, i: (0,) * len(shape))
    out_spec = pl.BlockSpec((None, TOKEN_TILE, w), lambda bi, i: (bi, i, 0))
    return pl.pallas_call(
        functools.partial(_rwkv_prep_kernel, n_ctx_tiles=n_ctx_tiles),
        grid=(b, nt),
        in_specs=[
            pl.BlockSpec((None, TOKEN_TILE, cw), lambda bi, i: (bi, i, 0)),
            pl.BlockSpec((None, SUBLANES, cw), lambda bi, i: (bi, jnp.maximum(i * hb - 1, 0), 0)),
            pl.BlockSpec((None, SUBLANES, cw), lambda bi, i: (bi, jnp.minimum((i + 1) * hb, t // SUBLANES - 1), 0)),
            full((3, cw)), full((3 * LANES, 4 * w)), full((1, 4 * w)),
            full((1, w)), full((1, w)), full((1, w)), full((2 * w, w)),
        ],
        out_specs=[out_spec] * 10,
        out_shape=[jax.ShapeDtypeStruct((b, t, w), F32)] * 10,
        scratch_shapes=[pltpu.VMEM((TOKEN_TILE + 2 * SUBLANES, cw), F32)],
        compiler_params=_cparams("parallel", "parallel"),
        name="rwkv_prep",
    )(ru, ru, ru, conv_w, wlr, b0, k_k, k_a, r_k, ones_heads)


def _split2(x):
    hi = x.astype(BF16)
    return hi, (x - hi.astype(F32)).astype(BF16)


def _head_sums(x, ones2):
    hi, lo = _split2(x)
    return _dot(jnp.concatenate([hi, lo], axis=1), ones2)


def _b(x):
    return x.astype(BF16)


def _bdot(a, b):
    return lax.dot_general(a, b, (((2,), (1,)), ((0,), (0,))), preferred_element_type=F32)


def _bdot_nt(a, b):
    return lax.dot_general(a, b, (((2,), (2,)), ((0,), (0,))), preferred_element_type=F32)


def _bdot_tn(a, b):
    return lax.dot_general(a, b, (((1,), (1,)), ((0,), (0,))), preferred_element_type=F32)


def _chunk_local(r, k, v, kap, bet, lw, consts):
    tri, rev, strict, incl, head_rows, eye2, eye_l = consts
    c = r.shape[1]
    l1 = lw.astype(BF16)
    l2 = (lw - l1.astype(F32)).astype(BF16)
    l3 = (lw - l1.astype(F32) - l2.astype(F32)).astype(BF16)
    lcs = _bdot(tri, jnp.concatenate([l1, l2, l3], axis=2))
    lc = lcs[:, :, 0:LANES] + lcs[:, :, LANES:2 * LANES] + lcs[:, :, 2 * LANES:3 * LANES]
    ltot = jnp.where(rev, lc[:, 0:1, :], lc[:, c - 1:c, :])
    e_in = jnp.exp(lc)
    e_inv = jnp.exp(-lc)
    e_hat = jnp.exp(ltot - lc)

    def stack(x):
        return jnp.where(head_rows, jnp.concatenate([x, x], axis=1), 0.0)

    r_s = stack(r * e_in)
    kap_s = stack(kap * jnp.exp(lc - lw))
    k_s = stack(k * e_inv)
    b_s = stack(bet * e_inv)
    kh_s = stack(k * e_hat)
    bh_s = stack(bet * e_hat)
    v_s = stack(v)

    m = _bdot_nt(_b(jnp.concatenate([kap_s, r_s], axis=1)), _b(jnp.concatenate([b_s, k_s], axis=1)))
    c2 = 2 * c
    a = jnp.where(strict, m[:, 0:c2, 0:c2], 0.0)
    bm = jnp.where(strict, m[:, 0:c2, c2:2 * c2], 0.0)
    mb = jnp.where(incl, m[:, c2:2 * c2, 0:c2], 0.0)
    mk = jnp.where(incl, m[:, c2:2 * c2, c2:2 * c2], 0.0)

    t_inv = eye2 - a
    pw = a
    for _ in range(int(np.log2(c)) - 1):
        pw_b = _b(_bdot(_b(pw), _b(pw)))
        pw = pw_b
        t_inv = t_inv + _bdot(_b(t_inv), pw_b)

    v_b = _b(v_s)
    bv = _bdot(_b(bm), v_b)
    x_b = _b(_bdot(_b(t_inv), _b(jnp.concatenate([kap_s, bv], axis=2))))
    z = _bdot(_b(mb), x_b)
    rp = r_s - z[:, :, 0:LANES]
    y0 = _bdot(_b(mk), v_b) - z[:, :, LANES:2 * LANES]
    bhx = _bdot_tn(_b(bh_s), x_b)
    g = jnp.where(eye_l, jnp.exp(ltot), 0.0) - bhx[:, :, 0:LANES]
    hadd = _bdot_tn(_b(kh_s), v_b) - bhx[:, :, LANES:2 * LANES]
    return rp, y0, g, hadd


def _rwkv_scan_kernel(rf, vf, kkf, lwf, kdf, bdf, rb, vb, kkb, lwb, kdb, bdb, yf_ref, yb_ref, h_ref):
    i = pl.program_id(1)

    @pl.when(i == 0)
    def _():
        h_ref[...] = jnp.zeros_like(h_ref)

    c = SCAN_CHUNK
    c2 = 2 * c
    n_slots = rf.shape[0] // c
    n_pairs = RWKV_W // LANES
    n_dir = 2 * n_pairs
    n = n_slots * n_dir
    iota = lambda shape, dim: lax.broadcasted_iota(jnp.int32, shape, dim)
    is_rev = lambda shape: lax.rem(iota(shape, 0), n_dir) >= n_pairs
    rev = is_rev((n, 1, 1))
    sign = lambda shape: jnp.where(is_rev(shape), 1, -1)
    tri = jnp.where((iota((n, c, c), 2) - iota((n, c, c), 1)) * sign((n, c, c)) >= 0, 1.0, 0.0).astype(BF16)
    row2, col2 = iota((n, c2, c2), 1), iota((n, c2, c2), 2)
    same = (row2 < c) == (col2 < c)
    rt = jnp.where(row2 < c, row2, row2 - c)
    ct = jnp.where(col2 < c, col2, col2 - c)
    before = (ct - rt) * sign((n, c2, c2))
    strict = jnp.logical_and(same, before > 0)
    incl = jnp.logical_and(same, before >= 0)
    head_rows = (iota((n, c2, LANES), 1) < c) == (iota((n, c2, LANES), 2) < HEAD_DIM)
    eye2 = jnp.where(row2 == col2, 1.0, 0.0)
    eye_l = iota((n, LANES, LANES), 1) == iota((n, LANES, LANES), 2)
    consts = (tri, rev, strict, incl, head_rows, eye2, eye_l)

    def rows(s, reverse):
        s = n_slots - 1 - s if reverse else s
        return slice(s * c, (s + 1) * c)

    def chains(fref, bref):
        return jnp.stack([ref[rows(s, ref is bref), p * LANES:(p + 1) * LANES]
                          for s in range(n_slots) for ref in (fref, bref) for p in range(n_pairs)], axis=0)

    rp, y0, g, hadd = _chunk_local(chains(rf, rb), chains(kdf, kdb), chains(vf, vb), chains(kkf, kkb),
                                   chains(bdf, bdb), chains(lwf, lwb), consts)
    h = h_ref[...]
    for s in range(n_slots):
        cs = slice(s * n_dir, (s + 1) * n_dir)
        h_b = _b(h)
        ys = _bdot(_b(rp[cs]), h_b) + y0[cs]
        h = _bdot(_b(g[cs]), h_b) + hadd[cs]
        y = ys[:, 0:c, :] + ys[:, c:c2, :]
        for p in range(n_pairs):
            yf_ref[rows(s, False), p * LANES:(p + 1) * LANES] = y[p]
            yb_ref[rows(s, True), p * LANES:(p + 1) * LANES] = y[n_pairs + p]
    h_ref[...] = h


def _rwkv_scan(prep, n_ctx):
    r, v, kk, lw0, kd0, bd0, lw1, kd1, bd1, _ = prep
    b, t, w = r.shape
    blk = SCAN_CHUNK * SCAN_CHUNKS_PER_STEP
    assert n_ctx % blk == 0 and t % blk == 0
    nb, ncb = t // blk, n_ctx // blk

    def bwd_block(i):
        return jnp.where(i < ncb, ncb - 1 - i, nb - 1 + ncb - i)

    fspec = pl.BlockSpec((None, blk, w), lambda bi, i: (bi, i, 0))
    bspec = pl.BlockSpec((None, blk, w), lambda bi, i: (bi, bwd_block(i), 0))
    return pl.pallas_call(
        _rwkv_scan_kernel,
        grid=(b, nb),
        in_specs=[fspec] * 6 + [bspec] * 6,
        out_specs=[fspec, bspec],
        out_shape=[jax.ShapeDtypeStruct((b, t, w), F32)] * 2,
        scratch_shapes=[pltpu.VMEM((2 * (w // LANES), LANES, LANES), F32)],
        compiler_params=_cparams("parallel", "arbitrary"),
        name="rwkv_scan",
    )(r, v, kk, lw0, kd0, bd0, r, v, kk, lw1, kd1, bd1)


def _qk_prep_kernel(aq_ref, ak_ref, av_ref, cos_ref, sa_ref, sb_ref, qn_ref, kn_ref, ones_ref, qt_o, k_o, vt_o):
    cos, sin_a, sin_b = cos_ref[...], sa_ref[...], sb_ref[...]
    ones = ones_ref[...]
    tm = cos.shape[0]
    row = lax.broadcasted_iota(jnp.int32, (LANES, tm), 0)

    def norm_rope(x, g):
        ms = _head_sums(x * x, ones) * (1.0 / HEAD_DIM)
        xn = x * lax.rsqrt(ms + RMS_EPS) * g
        return (xn * cos + pltpu.roll(xn, LANES - HEAD_DIM // 4, 1) * sin_a
                + pltpu.roll(xn, HEAD_DIM // 4, 1) * sin_b)

    for j in range(GQA_W // LANES):
        q = norm_rope(aq_ref[:, j * LANES:(j + 1) * LANES], qn_ref[...]) * (HEAD_DIM ** -0.5 * LOG2_E)
        qt = q.T
        qt_o[(2 * j) * LANES:(2 * j + 1) * LANES, :] = jnp.where(row < HEAD_DIM, qt, 0.0).astype(BF16)
        qt_o[(2 * j + 1) * LANES:(2 * j + 2) * LANES, :] = jnp.where(row >= HEAD_DIM, qt, 0.0).astype(BF16)
    k_o[...] = norm_rope(ak_ref[...], kn_ref[...]).astype(BF16)
    vt = av_ref[...].astype(F32).T
    one = jnp.ones((PV_ROWS - HEAD_DIM, tm), F32)
    vt_o[...] = jnp.concatenate([vt[0:HEAD_DIM], one, vt[HEAD_DIM:2 * HEAD_DIM], one], axis=0).astype(BF16)


def _qk_prep(aq, ak, av, cos, sin_a, sin_b, qn, kn, ones_pair, kv_chunk, n_ctx_tiles):
    b, t, _ = aq.shape
    nt = t // TOKEN_TILE
    per = kv_chunk // TOKEN_TILE
    tok = lambda width: pl.BlockSpec((None, TOKEN_TILE, width), lambda bi, i: (bi, i, 0))
    tab = pl.BlockSpec((TOKEN_TILE, LANES), lambda bi, i: (i, 0))
    full = lambda shape: pl.BlockSpec(shape, lambda bi, i: (0,) * len(shape))
    return pl.pallas_call(
        _qk_prep_kernel,
        grid=(b, nt),
        in_specs=[tok(GQA_W), tok(GQA_KV_W), tok(GQA_KV_W), tab, tab, tab,
                  full((1, LANES)), full((1, LANES)), full((2 * LANES, LANES))],
        out_specs=[
            pl.BlockSpec((None, 2 * GQA_W, TOKEN_TILE),
                         lambda bi, i: (bi, 0, jnp.where(i < n_ctx_tiles, i + nt - n_ctx_tiles, i - n_ctx_tiles))),
            tok(GQA_KV_W),
            pl.BlockSpec((None, None, GQA_KV_HEADS * PV_ROWS, TOKEN_TILE), lambda bi, i: (bi, i // per, 0, i % per)),
        ],
        out_shape=[
            jax.ShapeDtypeStruct((b, 2 * GQA_W, t), BF16),
            jax.ShapeDtypeStruct((b, t, GQA_KV_W), BF16),
            jax.ShapeDtypeStruct((b, t // kv_chunk, GQA_KV_HEADS * PV_ROWS, kv_chunk), BF16),
        ],
        compiler_params=_cparams("parallel", "parallel"),
        name="qk_prep",
    )(aq, ak, av, cos, sin_a, sin_b, qn, kn, ones_pair)


def _gqa_kernel(qt_ref, k_ref, vt_ref, o_ref, acc_ref):
    n_chunks = k_ref.shape[0]
    tq = qt_ref.shape[1]
    nj = GQA_W // LANES

    def q_slab(g):
        return jnp.concatenate([qt_ref[(2 * j + g) * LANES:(2 * j + g + 1) * LANES, :] for j in range(nj)], axis=1)

    def write_out():
        for j in range(nj):
            outs = []
            for g in range(GQA_KV_HEADS):
                acc = acc_ref[g, j]
                outs.append(acc[0:HEAD_DIM] / acc[HEAD_DIM:HEAD_DIM + 1])
            o_ref[:, j * LANES:(j + 1) * LANES] = jnp.concatenate(outs, axis=0).T

    def env_body(c, carry):
        kc = k_ref[c]
        return (jnp.maximum(carry[0], jnp.max(kc, axis=0, keepdims=True)),
                jnp.minimum(carry[1], jnp.min(kc, axis=0, keepdims=True)))

    k0 = k_ref[0]
    kmax, kmin = lax.fori_loop(1, n_chunks, env_body,
                               (jnp.max(k0, axis=0, keepdims=True), jnp.min(k0, axis=0, keepdims=True)))
    env = jnp.broadcast_to(jnp.concatenate([kmax, kmin], axis=1), (2 * SUBLANES, 2 * LANES))
    refs = []
    for g in range(GQA_KV_HEADS):
        qt = q_slab(g)
        zero = jnp.zeros_like(qt)
        refs.append(_dot(env, jnp.concatenate([jnp.maximum(qt, zero), jnp.minimum(qt, zero)], axis=0))[0:1, :])
    acc_ref[...] = jnp.zeros_like(acc_ref)

    def fast_body(c, carry):
        kc = k_ref[c]
        for g in range(GQA_KV_HEADS):
            p = jnp.exp2(_dot(kc, q_slab(g)) - refs[g]).astype(BF16)
            vt = vt_ref[c, g * PV_ROWS:(g + 1) * PV_ROWS, :]
            for j in range(nj):
                acc_ref[g, j] += _dot(vt, p[:, j * tq:(j + 1) * tq])
        return carry

    lax.fori_loop(0, n_chunks, fast_body, 0)
    low = acc_ref[0, 0][HEAD_DIM:HEAD_DIM + 1]
    for g in range(GQA_KV_HEADS):
        for j in range(nj):
            low = jnp.minimum(low, acc_ref[g, j][HEAD_DIM:HEAD_DIM + 1])
    fast_ok = jnp.min(low) >= FAST_PASS_MIN_DENOM

    @pl.when(fast_ok)
    def _():
        write_out()

    @pl.when(jnp.logical_not(fast_ok))
    def _():
        acc_ref[...] = jnp.zeros_like(acc_ref)

        def body(c, ms):
            kc = k_ref[c]
            new_ms = []
            for g in range(GQA_KV_HEADS):
                s = _dot(kc, q_slab(g))
                vt = vt_ref[c, g * PV_ROWS:(g + 1) * PV_ROWS, :]
                m_new = jnp.maximum(ms[g], jnp.max(s, axis=0, keepdims=True))
                alpha = jnp.exp2(ms[g] - m_new)
                p = jnp.exp2(s - m_new).astype(BF16)
                for j in range(nj):
                    sl = slice(j * tq, (j + 1) * tq)
                    acc_ref[g, j] = alpha[:, sl] * acc_ref[g, j] + _dot(vt, p[:, sl])
                new_ms.append(m_new)
            return tuple(new_ms)

        m0 = jnp.full((1, nj * tq), -jnp.inf, F32)
        lax.fori_loop(0, n_chunks, body, (m0,) * GQA_KV_HEADS)
        write_out()


def _gqa(qt, k4, vt4, q_start, n_q):
    b = qt.shape[0]
    _, n_chunks, kv_chunk, _ = k4.shape
    tq = next(w for w in GQA_Q_TILES if n_q % w == 0 and q_start % w == 0)
    q0 = q_start // tq
    return pl.pallas_call(
        _gqa_kernel,
        grid=(b, n_q // tq),
        in_specs=[
            pl.BlockSpec((None, 2 * GQA_W, tq), lambda bi, i: (bi, 0, i + q0)),
            pl.BlockSpec((None, n_chunks, kv_chunk, GQA_KV_W), lambda bi, i: (bi, 0, 0, 0)),
            pl.BlockSpec((None, n_chunks, GQA_KV_HEADS * PV_ROWS, kv_chunk), lambda bi, i: (bi, 0, 0, 0)),
        ],
        out_specs=pl.BlockSpec((None, tq, GQA_W), lambda bi, i: (bi, i, 0)),
        out_shape=jax.ShapeDtypeStruct((b, n_q, GQA_W), F32),
        scratch_shapes=[pltpu.VMEM((GQA_KV_HEADS, GQA_W // LANES, PV_ROWS, tq), F32)],
        compiler_params=_cparams("parallel", "parallel"),
        name="gqa",
    )(qt, k4, vt4)


def _natten_kernel(q_ref, k_ref, v_ref, bias_ref, o_ref, *, n_ctx, rows):
    i = pl.program_id(1)
    nq = GRID_W
    rows_per_step = q_ref.shape[0] // nq
    win = NA_WIN_ROWS * GRID_W
    is_ctx = i * rows_per_step < n_ctx // nq
    n_pairs = NA_W // LANES
    qs, kws, vws, kcs, vcs, bs = [], [], [], [], [], []
    for rr in range(rows_per_step):
        r = i * rows_per_step + rr - n_ctx // nq
        r_start = jnp.clip(r - NA_WIN_ROWS // 2, 0, rows - NA_WIN_ROWS)
        off = jnp.where(is_ctx, NA_WIN_ROWS, r_start - r + NA_WIN_ROWS - 1)
        koff = pl.multiple_of(n_ctx + r_start * GRID_W, GRID_W)
        for pair in range(n_pairs):
            sl = slice(pair * LANES, (pair + 1) * LANES)
            kw, vw = k_ref[pl.ds(koff, win), sl], v_ref[pl.ds(koff, win), sl]
            kc, vc = k_ref[0:n_ctx, sl], v_ref[0:n_ctx, sl]
            for half in range(2):
                hd = 2 * pair + half
                qs.append(q_ref[rr * nq:(rr + 1) * nq, hd * LANES:(hd + 1) * LANES])
                bs.append(bias_ref[hd, off])
                kws.append(kw)
                vws.append(vw)
                kcs.append(kc)
                vcs.append(vc)
    q = jnp.stack(qs, axis=0)
    s_w = _bdot_nt(q, jnp.stack(kws, axis=0)) + jnp.stack(bs, axis=0)
    s_c = _bdot_nt(q, jnp.stack(kcs, axis=0))
    m = jnp.maximum(jnp.max(s_w, axis=-1, keepdims=True), jnp.max(s_c, axis=-1, keepdims=True))
    p_w = jnp.exp(s_w - m)
    p_c = jnp.exp(s_c - m)
    l = jnp.sum(p_w, axis=-1, keepdims=True) + jnp.sum(p_c, axis=-1, keepdims=True)
    o = (_bdot(p_w.astype(BF16), jnp.stack(vws, axis=0)) + _bdot(p_c.astype(BF16), jnp.stack(vcs, axis=0))) / l
    lane = lax.broadcasted_iota(jnp.int32, (nq, LANES), 1)
    for rr in range(rows_per_step):
        for pair in range(n_pairs):
            e = (rr * n_pairs + pair) * 2
            o_ref[rr * nq:(rr + 1) * nq, pair * LANES:(pair + 1) * LANES] = jnp.where(lane < HEAD_DIM, o[e], o[e + 1])


def _natten(nq4, nk, nv, bias, n_ctx):
    b, t, _ = nk.shape
    rows = (t - n_ctx) // GRID_W
    assert rows >= NA_WIN_ROWS and n_ctx % TOKEN_TILE == 0 and TOKEN_TILE % GRID_W == 0
    kv = pl.BlockSpec((None, t, NA_W), lambda bi, i: (bi, 0, 0))
    return pl.pallas_call(
        functools.partial(_natten_kernel, n_ctx=n_ctx, rows=rows),
        grid=(b, t // TOKEN_TILE),
        in_specs=[
            pl.BlockSpec((None, TOKEN_TILE, NA_HEADS * LANES), lambda bi, i: (bi, i, 0)),
            kv, kv,
            pl.BlockSpec(bias.shape, lambda bi, i: (0, 0, 0, 0)),
        ],
        out_specs=pl.BlockSpec((None, TOKEN_TILE, NA_W), lambda bi, i: (bi, i, 0)),
        out_shape=jax.ShapeDtypeStruct((b, t, NA_W), F32),
        compiler_params=_cparams("parallel", "parallel"),
        name="natten",
    )(nq4, nk, nv, bias)


def _natten_bias(rpb):
    cols = np.arange(GRID_W)
    c_start = np.clip(cols - NA_WIN_COLS // 2, 0, GRID_W - NA_WIN_COLS)
    key = np.arange(GRID_W)
    valid = (key[None, :] >= c_start[:, None]) & (key[None, :] < c_start[:, None] + NA_WIN_COLS)
    dc = key[None, :] - cols[:, None] + NA_WIN_COLS - 1
    place = (valid[:, :, None] & (dc[:, :, None] == np.arange(2 * NA_WIN_COLS - 1))).astype(np.float32)
    mask = np.where(valid, 0.0, MASK_BIAS).astype(np.float32)
    rows = jnp.stack([rpb[:, o:o + NA_WIN_ROWS, :] for o in range(NA_WIN_ROWS)], axis=1)
    dense = jnp.einsum("hopd,cxd->hocpx", rows, jnp.asarray(place), precision=HI) + mask[None, None, :, None, :]
    dense = dense.reshape(NA_HEADS, NA_WIN_ROWS, GRID_W, NA_WIN_ROWS * GRID_W)
    return jnp.concatenate([dense, jnp.full_like(dense[:, :1], MASK_BIAS)], axis=1).astype(F32)


def _out_kernel(x_ref, yf_ref, yb_ref, bonus_ref, ya_ref, yn_ref, g_ref, mod_ref, w_ref,
                gng_ref, gnb_ref, lng_ref, lnb_ref, ones_ref, o_ref, *, alpha):
    ones = ones_ref[...]
    ys = yf_ref[...] + yb_ref[...]
    mu = _head_sums(ys, ones) * (1.0 / HEAD_DIM)
    dv = ys - mu
    var = _head_sums(dv * dv, ones) * (1.0 / HEAD_DIM)
    yr = dv * lax.rsqrt(var + GN_EPS) * gng_ref[...] + gnb_ref[...] + bonus_ref[...]
    g = g_ref[...].astype(F32)
    parts = (
        yr * _silu(g[:, 0:RWKV_W]),
        ya_ref[...] * _silu(g[:, RWKV_W:RWKV_W + GQA_W]),
        yn_ref[...] * _silu(g[:, RWKV_W + GQA_W:]),
    )
    mixed = jnp.concatenate([p.astype(BF16) for p in parts], axis=1)
    y = _dot(mixed, w_ref[...])
    xn = alpha * x_ref[...] + mod_ref[2:3, :] * y
    mean = jnp.mean(xn, axis=-1, keepdims=True)
    xc = xn - mean
    var = jnp.mean(xc * xc, axis=-1, keepdims=True)
    o_ref[...] = xc * lax.rsqrt(var + LN_EPS) * lng_ref[...] + lnb_ref[...]


def _output(xc, yf, yb, bonus, ya, yn, gates, modsel, w_out, gn_g, gn_b, ln_g, ln_b, ones_heads, alpha, n_ctx_tiles,
            skip_tiles):
    b, t, d = xc.shape
    nt = t // TOKEN_TILE - skip_tiles
    tok = lambda width: pl.BlockSpec((None, TOKEN_TILE, width), lambda bi, i: (bi, i + skip_tiles, 0))
    full = lambda shape: pl.BlockSpec(shape, lambda bi, i: (0,) * len(shape))
    return pl.pallas_call(
        functools.partial(_out_kernel, alpha=alpha),
        grid=(b, nt),
        in_specs=[
            tok(d), tok(RWKV_W), tok(RWKV_W), tok(RWKV_W), tok(GQA_W), tok(NA_W), tok(RWKV_W + GQA_W + NA_W),
            pl.BlockSpec((None, None, 3, d), lambda bi, i: (bi, jnp.where(i + skip_tiles < n_ctx_tiles, 0, 1), 0, 0)),
            full(w_out.shape), full((1, RWKV_W)), full((1, RWKV_W)), full((1, d)), full((1, d)), full((2 * RWKV_W, RWKV_W)),
        ],
        out_specs=pl.BlockSpec((None, TOKEN_TILE, d), lambda bi, i: (bi, i, 0)),
        out_shape=jax.ShapeDtypeStruct((b, nt * TOKEN_TILE, d), F32),
        compiler_params=_cparams("parallel", "parallel"),
        name="out",
    )(xc, yf, yb, bonus, ya, yn, gates, modsel, w_out, gn_g, gn_b, ln_g, ln_b, ones_heads)


def _rope_tables(n_ctx, n_lat):
    t = jnp.arange(n_lat, dtype=jnp.int32)
    row = (t // GRID_W).astype(F32)
    col = (t % GRID_W).astype(F32)
    n_freq = HEAD_DIM // 4
    inv_freq = ROPE_THETA ** (-jnp.arange(n_freq, dtype=F32) / n_freq)
    ang_r = row[:, None] * inv_freq
    ang_c = col[:, None] * inv_freq
    ang = jnp.concatenate([ang_r, ang_r, ang_c, ang_c], axis=-1)
    cos = jnp.concatenate([jnp.ones((n_ctx, HEAD_DIM), F32), jnp.cos(ang)], axis=0)
    sin = jnp.concatenate([jnp.zeros((n_ctx, HEAD_DIM), F32), jnp.sin(ang)], axis=0)
    first = (np.arange(HEAD_DIM) % (HEAD_DIM // 2)) < HEAD_DIM // 4
    sin_a = jnp.where(first, -sin, 0.0)
    sin_b = jnp.where(first, 0.0, sin)
    pair = lambda z: jnp.concatenate([z, z], axis=-1)
    return pair(cos), pair(sin_a), pair(sin_b)


def _block_ones(n_heads):
    blk = np.kron(np.eye(n_heads, dtype=np.float32), np.ones((HEAD_DIM, HEAD_DIM), np.float32))
    return jnp.asarray(blk)


def kernel(x, c, ctx, c_ctx, w_mod, b_mod, w_in, w_out, rwkv_conv, decay_w0, decay_w2, iclr_a0, iclr_a2, rwkv_k_k, rwkv_k_a, rwkv_r_k, rwkv_gn_g, rwkv_gn_b, gqa_q_norm, gqa_k_norm, na_rpb, ln_g, ln_b):
    b, n_lat, d = x.shape
    n_ctx = ctx.shape[1]
    depth = w_mod.shape[0]
    alpha = float((2 * depth) ** 0.25)
    n_ctx_tiles = n_ctx // TOKEN_TILE
    assert n_ctx % TOKEN_TILE == 0 and n_lat % TOKEN_TILE == 0 and n_lat % GRID_W == 0

    n_rows = -(-(b + 1) // SUBLANES) * SUBLANES
    cvec = jnp.zeros((n_rows, d), F32).at[:b].set(c).at[b].set(c_ctx)
    mod = _modulation(cvec, w_mod, b_mod).reshape(depth, n_rows, 3, d)
    mod_ctx = jnp.broadcast_to(mod[:, b][:, None], (depth, b, 3, d))
    modsel = jnp.stack([mod_ctx, mod[:, :b]], axis=2)

    sizes = (RWKV_CONV_W, RWKV_W, GQA_W, GQA_KV_W, GQA_KV_W, GQA_W, NA_W, NA_W, NA_W, NA_W)
    offs = np.concatenate([[0], np.cumsum(sizes)])
    seg = lambda n: w_in[:, :, offs[n]:offs[n + 1]]

    def pair_heads(z, axis):
        shp = z.shape[:axis] + (GQA_KV_HEADS, GQA_Q_HEADS // GQA_KV_HEADS, HEAD_DIM) + z.shape[axis + 1:]
        return jnp.swapaxes(z.reshape(shp), axis, axis + 1).reshape(z.shape)

    w_in_p = jnp.concatenate([seg(0), pair_heads(seg(2), 2), seg(3), seg(4), seg(6), seg(7), seg(8),
                              seg(1), pair_heads(seg(5), 2), seg(9)], axis=2).astype(BF16)
    w_out_p = jnp.concatenate([w_out[:, :RWKV_W], pair_heads(w_out[:, RWKV_W:RWKV_W + GQA_W], 1),
                               w_out[:, RWKV_W + GQA_W:]], axis=1).astype(BF16)

    w = RWKV_W
    zero = jnp.zeros((depth, LOW_RANK, w), F32)
    blocks = (decay_w2[:, 0], decay_w2[:, 1], iclr_a2[:, 0], iclr_a2[:, 1])
    wlr = jnp.concatenate([jnp.concatenate([blk if cc == rr else zero for cc in range(4)], axis=2)
                           for rr, blk in enumerate(blocks)], axis=1)
    wlr_hi = wlr.astype(BF16)
    wlr_lo = (wlr - wlr_hi.astype(F32)).astype(BF16)
    wlr3 = jnp.concatenate([wlr_hi, wlr_hi, wlr_lo], axis=1)
    b0 = jnp.concatenate([decay_w0[:, 0], decay_w0[:, 1], iclr_a0[:, 0], iclr_a0[:, 1]], axis=-1)[:, None, :]

    cos, sin_a, sin_b = _rope_tables(n_ctx, n_lat)
    ones_heads = jnp.tile(_block_ones(RWKV_HEADS), (2, 1)).astype(BF16)
    ones_pair = jnp.tile(_block_ones(2), (2, 1)).astype(BF16)
    pair = lambda z: jnp.concatenate([z, z], axis=-1)[:, None, :]
    qn, kn = pair(gqa_q_norm), pair(gqa_k_norm)

    xc = jnp.concatenate([ctx, x], axis=1)
    t = n_ctx + n_lat
    kv_chunk = next(ch for ch in KV_CHUNKS if t % ch == 0)
    for l in range(depth):
        ru, aq, ak, av, nq4, nk, nv, gates = _projection(xc, modsel[l], w_in_p[l], n_ctx_tiles)
        prep = _rwkv_prep(ru, rwkv_conv[l], wlr3[l], b0[l], rwkv_k_k[l][None], rwkv_k_a[l][None],
                          rwkv_r_k[l].reshape(1, w), ones_heads, n_ctx_tiles)
        yf, yb = _rwkv_scan(prep, n_ctx)
        last = l == depth - 1
        qt, kr, vt4 = _qk_prep(aq, ak, av, cos, sin_a, sin_b, qn[l], kn[l], ones_pair, kv_chunk, n_ctx_tiles)
        k4 = kr.reshape(b, t // kv_chunk, kv_chunk, GQA_KV_W)
        ya = _gqa(qt, k4, vt4, 0, n_lat)
        if last:
            ya = jnp.concatenate([jnp.zeros((b, n_ctx, GQA_W), F32), ya], axis=1)
        else:
            ya_c = _gqa(qt, kr[:, :n_ctx].reshape(b, 1, n_ctx, GQA_KV_W), vt4[:, :1, :, :n_ctx], n_lat, n_ctx)
            ya = jnp.concatenate([ya_c, ya], axis=1)
        yn = _natten(nq4, nk, nv, _natten_bias(na_rpb[l]), n_ctx)
        xc = _output(xc, yf, yb, prep[9], ya, yn, gates, modsel[l], w_out_p[l], rwkv_gn_g[l][None], rwkv_gn_b[l][None],
                     ln_g[l][None], ln_b[l][None], ones_heads, alpha, n_ctx_tiles, n_ctx_tiles if last else 0)
    return xc
```

```python
import functools

import numpy as np
import jax
import jax.numpy as jnp
from jax import lax
from jax.experimental import pallas as pl
from jax.experimental.pallas import tpu as pltpu

F32 = jnp.float32
BF16 = jnp.bfloat16
HI = lax.Precision.HIGHEST

HEAD_DIM = 64
RWKV_HEADS = 6
GQA_Q_HEADS = 6
GQA_KV_HEADS = 2
NA_HEADS = 4
RWKV_W = RWKV_HEADS * HEAD_DIM
GQA_W = GQA_Q_HEADS * HEAD_DIM
GQA_KV_W = GQA_KV_HEADS * HEAD_DIM
NA_W = NA_HEADS * HEAD_DIM
LOW_RANK = 32
RWKV_CONV_W = 3 * RWKV_W + 4 * LOW_RANK
GRID_W = 64
NA_WIN_ROWS = 8
NA_WIN_COLS = 16
ROPE_THETA = 10000.0
LN_EPS = 1e-5
RMS_EPS = 1e-6
GN_EPS = 64e-5
KK_EPS = 1e-12
MASK_BIAS = -1e30

LANES = 128
SUBLANES = 8
TOKEN_TILE = 256
SCAN_CHUNK = 64
SCAN_CHUNKS_PER_STEP = 4
KV_CHUNKS = (768, 512, 256)
GQA_Q_TILES = (512, 256)
PV_ROWS = HEAD_DIM + 16
FAST_PASS_MIN_DENOM = 2.0 ** -64
LOG2_E = 1.4426950408889634
VMEM_LIMIT = 56 * 1024 * 1024


def _cparams(*sem):
    return pltpu.CompilerParams(dimension_semantics=sem, vmem_limit_bytes=VMEM_LIMIT)


def _dot(a, b, precision=None):
    return jnp.dot(a, b, preferred_element_type=F32, precision=precision)


def _dot_nt(a, b, precision=None):
    return lax.dot_general(a, b, (((1,), (1,)), ((), ())), preferred_element_type=F32, precision=precision)


def _dot_tn(a, b, precision=None):
    return lax.dot_general(a, b, (((0,), (0,)), ((), ())), preferred_element_type=F32, precision=precision)


def _sigmoid(x):
    return 1.0 / (1.0 + jnp.exp(-x))


def _silu(x):
    return x * _sigmoid(x)


def _mod_kernel(c_ref, w_ref, b_ref, o_ref):
    o_ref[...] = _dot(_silu(c_ref[...]), w_ref[...], HI) + b_ref[...]


def _modulation(cvec, w_mod, b_mod):
    depth, d, d3 = w_mod.shape
    r = cvec.shape[0]
    nj = d3 // d
    return pl.pallas_call(
        _mod_kernel,
        grid=(depth, nj),
        in_specs=[
            pl.BlockSpec((r, d), lambda l, j: (0, 0)),
            pl.BlockSpec((None, d, d), lambda l, j: (l, 0, j)),
            pl.BlockSpec((None, 1, d), lambda l, j: (l, 0, j)),
        ],
        out_specs=pl.BlockSpec((None, r, d), lambda l, j: (l, 0, j)),
        out_shape=jax.ShapeDtypeStruct((depth, r, d3), F32),
        compiler_params=_cparams("parallel", "parallel"),
        name="modulation",
    )(cvec, w_mod, b_mod.reshape(depth, 1, d3))


_PROJ_SLABS = (
    ("ru", RWKV_CONV_W, F32),
    ("aq", GQA_W, F32),
    ("ak", GQA_KV_W, F32),
    ("av", GQA_KV_W, BF16),
    ("nq", NA_W, None),
    ("nk", NA_W, BF16),
    ("nv", NA_W, BF16),
    ("gates", RWKV_W + GQA_W + NA_W, BF16),
)


def _proj_kernel(x_ref, mod_ref, w_ref, ru_ref, aq_ref, ak_ref, av_ref, nq_ref, nk_ref, nv_ref, g_ref):
    shift = mod_ref[0:1, :]
    scale = mod_ref[1:2, :]
    h = (x_ref[...] * (1.0 + scale) + shift).astype(BF16)
    outs = {"ru": ru_ref, "aq": aq_ref, "ak": ak_ref, "av": av_ref, "nk": nk_ref, "nv": nv_ref, "gates": g_ref}
    off = 0
    for name, width, _ in _PROJ_SLABS:
        u = _dot(h, w_ref[:, off:off + width])
        if name == "nq":
            u = u * (HEAD_DIM ** -0.5)
            lane = lax.broadcasted_iota(jnp.int32, (u.shape[0], LANES), 1)
            for hd in range(NA_HEADS):
                pair = u[:, (hd // 2) * LANES:(hd // 2 + 1) * LANES]
                keep = (lane < HEAD_DIM) if hd % 2 == 0 else (lane >= HEAD_DIM)
                nq_ref[:, hd * LANES:(hd + 1) * LANES] = jnp.where(keep, pair, 0.0).astype(BF16)
        else:
            outs[name][...] = u.astype(outs[name].dtype)
        off += width


def _projection(xc, modsel, w_perm, n_ctx_tiles):
    b, t, d = xc.shape
    nt = t // TOKEN_TILE
    n_in = w_perm.shape[1]
    shapes, specs = [], []
    for name, width, dt in _PROJ_SLABS:
        if name == "nq":
            width, dt = NA_HEADS * LANES, BF16
        shapes.append(jax.ShapeDtypeStruct((b, t, width), dt))
        specs.append(pl.BlockSpec((None, TOKEN_TILE, width), lambda bi, i: (bi, i, 0)))
    return pl.pallas_call(
        _proj_kernel,
        grid=(b, nt),
        in_specs=[
            pl.BlockSpec((None, TOKEN_TILE, d), lambda bi, i: (bi, i, 0)),
            pl.BlockSpec((None, None, 3, d), lambda bi, i: (bi, jnp.where(i < n_ctx_tiles, 0, 1), 0, 0)),
            pl.BlockSpec((d, n_in), lambda bi, i: (0, 0)),
        ],
        out_specs=specs,
        out_shape=shapes,
        compiler_params=_cparams("parallel", "parallel"),
        name="proj",
    )(xc, modsel, w_perm)


def _rwkv_prep_kernel(x_ref, xp_ref, xn_ref, cw_ref, wlr_ref, b0_ref, kk_ref, ka_ref, rk_ref, ones_ref,
                      r_o, v_o, kk_o, lw0_o, kd0_o, bd0_o, lw1_o, kd1_o, bd1_o, bonus_o, *, n_ctx_tiles):
    i = pl.program_id(1)
    nt = pl.num_programs(1)
    tm = x_ref.shape[0]
    prev_ok = jnp.logical_and(i != 0, i != n_ctx_tiles)
    next_ok = jnp.logical_and(i != n_ctx_tiles - 1, i != nt - 1)
    x = x_ref[...]
    row = lax.broadcasted_iota(jnp.int32, x.shape, 0)
    halo_prev = jnp.where(prev_ok, xp_ref[SUBLANES - 1:SUBLANES, :], 0.0)
    halo_next = jnp.where(next_ok, xn_ref[0:1, :], 0.0)
    x_prev = jnp.where(row == 0, halo_prev, pltpu.roll(x, 1, 0))
    x_next = jnp.where(row == tm - 1, halo_next, pltpu.roll(x, tm - 1, 0))
    u = cw_ref[0:1, :] * x_prev + cw_ref[1:2, :] * x + cw_ref[2:3, :] * x_next

    w = RWKV_W
    r, k, v = u[:, 0:w], u[:, w:2 * w], u[:, 2 * w:3 * w]
    lr = u[:, 3 * w:3 * w + LANES]
    lane = lax.broadcasted_iota(jnp.int32, lr.shape, 1)
    z = jnp.where(lane < 2 * LOW_RANK, jnp.tanh(lr), lr)
    z_hi, z_lo = _split2(z)
    pre = _dot(jnp.concatenate([z_hi, z_lo, z_hi], axis=1), wlr_ref[...]) + b0_ref[...]

    ones = ones_ref[...]
    kk0 = k * kk_ref[...]
    ss = _head_sums(kk0 * kk0, ones)
    kk = kk0 * lax.rsqrt(jnp.maximum(ss, KK_EPS))
    r_o[...] = r
    v_o[...] = v
    kk_o[...] = kk
    kd_sum = None
    for d, (lw_o, kd_o, bd_o) in enumerate(((lw0_o, kd0_o, bd0_o), (lw1_o, kd1_o, bd1_o))):
        xd = -pre[:, d * w:(d + 1) * w]
        softplus = jnp.maximum(xd, 0.0) + jnp.log(1.0 + jnp.exp(-jnp.abs(xd)))
        lw_o[...] = -jnp.exp(-softplus - 0.5)
        iclr = _sigmoid(pre[:, (2 + d) * w:(3 + d) * w])
        kd = k * (1.0 + (iclr - 1.0) * ka_ref[...])
        kd_o[...] = kd
        bd_o[...] = kk * iclr
        kd_sum = kd if kd_sum is None else kd_sum + kd
    bonus_o[...] = _head_sums(r * kd_sum * rk_ref[...], ones) * v


def _rwkv_prep(ru, conv_w, wlr, b0, k_k, k_a, r_k, ones_heads, n_ctx_tiles):
    b, t, cw = ru.shape
    nt = t // TOKEN_TILE
    hb = TOKEN_TILE // SUBLANES
    w = RWKV_W
    full = lambda shape: pl.BlockSpec(shape, lambda bi, i: (0,) * len(shape))
    out_spec = pl.BlockSpec((None, TOKEN_TILE, w), lambda bi, i: (bi, i, 0))
    return pl.pallas_call(
        functools.partial(_rwkv_prep_kernel, n_ctx_tiles=n_ctx_tiles),
        grid=(b, nt),
        in_specs=[
            pl.BlockSpec((None, TOKEN_TILE, cw), lambda bi, i: (bi, i, 0)),
            pl.BlockSpec((None, SUBLANES, cw), lambda bi, i: (bi, jnp.maximum(i * hb - 1, 0), 0)),
            pl.BlockSpec((None, SUBLANES, cw), lambda bi, i: (bi, jnp.minimum((i + 1) * hb, t // SUBLANES - 1), 0)),
            full((3, cw)), full((3 * LANES, 4 * w)), full((1, 4 * w)),
            full((1, w)), full((1, w)), full((1, w)), full((2 * w, w)),
        ],
        out_specs=[out_spec] * 10,
        out_shape=[jax.ShapeDtypeStruct((b, t, w), F32)] * 10,
        compiler_params=_cparams("parallel", "parallel"),
        name="rwkv_prep",
    )(ru, ru, ru, conv_w, wlr, b0, k_k, k_a, r_k, ones_heads)


def _split2(x):
    hi = x.astype(BF16)
    return hi, (x - hi.astype(F32)).astype(BF16)


def _head_sums(x, ones2):
    hi, lo = _split2(x)
    return _dot(jnp.concatenate([hi, lo], axis=1), ones2)


def _b(x):
    return x.astype(BF16)


def _bdot(a, b):
    return lax.dot_general(a, b, (((2,), (1,)), ((0,), (0,))), preferred_element_type=F32)


def _bdot_nt(a, b):
    return lax.dot_general(a, b, (((2,), (2,)), ((0,), (0,))), preferred_element_type=F32)


def _bdot_tn(a, b):
    return lax.dot_general(a, b, (((1,), (1,)), ((0,), (0,))), preferred_element_type=F32)


def _chunk_local(r, k, v, kap, bet, lw, consts):
    tri, rev, strict, incl, head_rows, eye2, eye_l = consts
    c = r.shape[1]
    l1 = lw.astype(BF16)
    l2 = (lw - l1.astype(F32)).astype(BF16)
    l3 = (lw - l1.astype(F32) - l2.astype(F32)).astype(BF16)
    lcs = _bdot(tri, jnp.concatenate([l1, l2, l3], axis=2))
    lc = lcs[:, :, 0:LANES] + lcs[:, :, LANES:2 * LANES] + lcs[:, :, 2 * LANES:3 * LANES]
    ltot = jnp.where(rev, lc[:, 0:1, :], lc[:, c - 1:c, :])
    e_in = jnp.exp(lc)
    e_inv = jnp.exp(-lc)
    e_hat = jnp.exp(ltot - lc)

    def stack(x):
        return jnp.where(head_rows, jnp.concatenate([x, x], axis=1), 0.0)

    r_s = stack(r * e_in)
    kap_s = stack(kap * jnp.exp(lc - lw))
    k_s = stack(k * e_inv)
    b_s = stack(bet * e_inv)
    kh_s = stack(k * e_hat)
    bh_s = stack(bet * e_hat)
    v_s = stack(v)

    m = _bdot_nt(_b(jnp.concatenate([kap_s, r_s], axis=1)), _b(jnp.concatenate([b_s, k_s], axis=1)))
    c2 = 2 * c
    a = jnp.where(strict, m[:, 0:c2, 0:c2], 0.0)
    bm = jnp.where(strict, m[:, 0:c2, c2:2 * c2], 0.0)
    mb = jnp.where(incl, m[:, c2:2 * c2, 0:c2], 0.0)
    mk = jnp.where(incl, m[:, c2:2 * c2, c2:2 * c2], 0.0)

    t_inv = eye2 - a
    pw = a
    for _ in range(int(np.log2(c)) - 1):
        pw_b = _b(_bdot(_b(pw), _b(pw)))
        pw = pw_b
        t_inv = t_inv + _bdot(_b(t_inv), pw_b)

    v_b = _b(v_s)
    bv = _bdot(_b(bm), v_b)
    x_b = _b(_bdot(_b(t_inv), _b(jnp.concatenate([kap_s, bv], axis=2))))
    z = _bdot(_b(mb), x_b)
    rp = r_s - z[:, :, 0:LANES]
    y0 = _bdot(_b(mk), v_b) - z[:, :, LANES:2 * LANES]
    bhx = _bdot_tn(_b(bh_s), x_b)
    g = jnp.where(eye_l, jnp.exp(ltot), 0.0) - bhx[:, :, 0:LANES]
    hadd = _bdot_tn(_b(kh_s), v_b) - bhx[:, :, LANES:2 * LANES]
    return rp, y0, g, hadd


def _rwkv_scan_kernel(rf, vf, kkf, lwf, kdf, bdf, rb, vb, kkb, lwb, kdb, bdb, yf_ref, yb_ref, h_ref):
    i = pl.program_id(1)

    @pl.when(i == 0)
    def _():
        h_ref[...] = jnp.zeros_like(h_ref)

    c = SCAN_CHUNK
    c2 = 2 * c
    n_slots = rf.shape[0] // c
    n_pairs = RWKV_W // LANES
    n_dir = 2 * n_pairs
    n = n_slots * n_dir
    iota = lambda shape, dim: lax.broadcasted_iota(jnp.int32, shape, dim)
    is_rev = lambda shape: lax.rem(iota(shape, 0), n_dir) >= n_pairs
    rev = is_rev((n, 1, 1))
    sign = lambda shape: jnp.where(is_rev(shape), 1, -1)
    tri = jnp.where((iota((n, c, c), 2) - iota((n, c, c), 1)) * sign((n, c, c)) >= 0, 1.0, 0.0).astype(BF16)
    row2, col2 = iota((n, c2, c2), 1), iota((n, c2, c2), 2)
    same = (row2 < c) == (col2 < c)
    rt = jnp.where(row2 < c, row2, row2 - c)
    ct = jnp.where(col2 < c, col2, col2 - c)
    before = (ct - rt) * sign((n, c2, c2))
    strict = jnp.logical_and(same, before > 0)
    incl = jnp.logical_and(same, before >= 0)
    head_rows = (iota((n, c2, LANES), 1) < c) == (iota((n, c2, LANES), 2) < HEAD_DIM)
    eye2 = jnp.where(row2 == col2, 1.0, 0.0)
    eye_l = iota((n, LANES, LANES), 1) == iota((n, LANES, LANES), 2)
    consts = (tri, rev, strict, incl, head_rows, eye2, eye_l)

    def rows(s, reverse):
        s = n_slots - 1 - s if reverse else s
        return slice(s * c, (s + 1) * c)

    def chains(fref, bref):
        return jnp.stack([ref[rows(s, ref is bref), p * LANES:(p + 1) * LANES]
                          for s in range(n_slots) for ref in (fref, bref) for p in range(n_pairs)], axis=0)

    rp, y0, g, hadd = _chunk_local(chains(rf, rb), chains(kdf, kdb), chains(vf, vb), chains(kkf, kkb),
                                   chains(bdf, bdb), chains(lwf, lwb), consts)
    h = h_ref[...]
    for s in range(n_slots):
        cs = slice(s * n_dir, (s + 1) * n_dir)
        h_b = _b(h)
        ys = _bdot(_b(rp[cs]), h_b) + y0[cs]
        h = _bdot(_b(g[cs]), h_b) + hadd[cs]
        y = ys[:, 0:c, :] + ys[:, c:c2, :]
        for p in range(n_pairs):
            yf_ref[rows(s, False), p * LANES:(p + 1) * LANES] = y[p]
            yb_ref[rows(s, True), p * LANES:(p + 1) * LANES] = y[n_pairs + p]
    h_ref[...] = h


def _rwkv_scan(prep, n_ctx):
    r, v, kk, lw0, kd0, bd0, lw1, kd1, bd1, _ = prep
    b, t, w = r.shape
    blk = SCAN_CHUNK * SCAN_CHUNKS_PER_STEP
    assert n_ctx % blk == 0 and t % blk == 0
    nb, ncb = t // blk, n_ctx // blk

    def bwd_block(i):
        return jnp.where(i < ncb, ncb - 1 - i, nb - 1 + ncb - i)

    fspec = pl.BlockSpec((None, blk, w), lambda bi, i: (bi, i, 0))
    bspec = pl.BlockSpec((None, blk, w), lambda bi, i: (bi, bwd_block(i), 0))
    return pl.pallas_call(
        _rwkv_scan_kernel,
        grid=(b, nb),
        in_specs=[fspec] * 6 + [bspec] * 6,
        out_specs=[fspec, bspec],
        out_shape=[jax.ShapeDtypeStruct((b, t, w), F32)] * 2,
        scratch_shapes=[pltpu.VMEM((2 * (w // LANES), LANES, LANES), F32)],
        compiler_params=_cparams("parallel", "arbitrary"),
        name="rwkv_scan",
    )(r, v, kk, lw0, kd0, bd0, r, v, kk, lw1, kd1, bd1)


def _qk_prep_kernel(aq_ref, ak_ref, av_ref, cos_ref, sa_ref, sb_ref, qn_ref, kn_ref, ones_ref, qt_o, k_o, vt_o):
    cos, sin_a, sin_b = cos_ref[...], sa_ref[...], sb_ref[...]
    ones = ones_ref[...]
    tm = cos.shape[0]
    row = lax.broadcasted_iota(jnp.int32, (LANES, tm), 0)

    def norm_rope(x, g):
        ms = _head_sums(x * x, ones) * (1.0 / HEAD_DIM)
        xn = x * lax.rsqrt(ms + RMS_EPS) * g
        return (xn * cos + pltpu.roll(xn, LANES - HEAD_DIM // 4, 1) * sin_a
                + pltpu.roll(xn, HEAD_DIM // 4, 1) * sin_b)

    for j in range(GQA_W // LANES):
        q = norm_rope(aq_ref[:, j * LANES:(j + 1) * LANES], qn_ref[...]) * (HEAD_DIM ** -0.5 * LOG2_E)
        qt = q.T
        qt_o[(2 * j) * LANES:(2 * j + 1) * LANES, :] = jnp.where(row < HEAD_DIM, qt, 0.0).astype(BF16)
        qt_o[(2 * j + 1) * LANES:(2 * j + 2) * LANES, :] = jnp.where(row >= HEAD_DIM, qt, 0.0).astype(BF16)
    k_o[...] = norm_rope(ak_ref[...], kn_ref[...]).astype(BF16)
    vt = av_ref[...].astype(F32).T
    one = jnp.ones((PV_ROWS - HEAD_DIM, tm), F32)
    vt_o[...] = jnp.concatenate([vt[0:HEAD_DIM], one, vt[HEAD_DIM:2 * HEAD_DIM], one], axis=0).astype(BF16)


def _qk_prep(aq, ak, av, cos, sin_a, sin_b, qn, kn, ones_pair, kv_chunk, n_ctx_tiles):
    b, t, _ = aq.shape
    nt = t // TOKEN_TILE
    per = kv_chunk // TOKEN_TILE
    tok = lambda width: pl.BlockSpec((None, TOKEN_TILE, width), lambda bi, i: (bi, i, 0))
    tab = pl.BlockSpec((TOKEN_TILE, LANES), lambda bi, i: (i, 0))
    full = lambda shape: pl.BlockSpec(shape, lambda bi, i: (0,) * len(shape))
    return pl.pallas_call(
        _qk_prep_kernel,
        grid=(b, nt),
        in_specs=[tok(GQA_W), tok(GQA_KV_W), tok(GQA_KV_W), tab, tab, tab,
                  full((1, LANES)), full((1, LANES)), full((2 * LANES, LANES))],
        out_specs=[
            pl.BlockSpec((None, 2 * GQA_W, TOKEN_TILE),
                         lambda bi, i: (bi, 0, jnp.where(i < n_ctx_tiles, i + nt - n_ctx_tiles, i - n_ctx_tiles))),
            tok(GQA_KV_W),
            pl.BlockSpec((None, None, GQA_KV_HEADS * PV_ROWS, TOKEN_TILE), lambda bi, i: (bi, i // per, 0, i % per)),
        ],
        out_shape=[
            jax.ShapeDtypeStruct((b, 2 * GQA_W, t), BF16),
            jax.ShapeDtypeStruct((b, t, GQA_KV_W), BF16),
            jax.ShapeDtypeStruct((b, t // kv_chunk, GQA_KV_HEADS * PV_ROWS, kv_chunk), BF16),
        ],
        compiler_params=_cparams("parallel", "parallel"),
        name="qk_prep",
    )(aq, ak, av, cos, sin_a, sin_b, qn, kn, ones_pair)


def _gqa_kernel(qt_ref, k_ref, vt_ref, o_ref, acc_ref):
    n_chunks = k_ref.shape[0]
    tq = qt_ref.shape[1]
    nj = GQA_W // LANES

    def q_slab(g):
        return jnp.concatenate([qt_ref[(2 * j + g) * LANES:(2 * j + g + 1) * LANES, :] for j in range(nj)], axis=1)

    def write_out():
        for j in range(nj):
            outs = []
            for g in range(GQA_KV_HEADS):
                acc = acc_ref[g, j]
                outs.append(acc[0:HEAD_DIM] / acc[HEAD_DIM:HEAD_DIM + 1])
            o_ref[:, j * LANES:(j + 1) * LANES] = jnp.concatenate(outs, axis=0).T

    def env_body(c, carry):
        kc = k_ref[c]
        return (jnp.maximum(carry[0], jnp.max(kc, axis=0, keepdims=True)),
                jnp.minimum(carry[1], jnp.min(kc, axis=0, keepdims=True)))

    k0 = k_ref[0]
    kmax, kmin = lax.fori_loop(1, n_chunks, env_body,
                               (jnp.max(k0, axis=0, keepdims=True), jnp.min(k0, axis=0, keepdims=True)))
    env = jnp.broadcast_to(jnp.concatenate([kmax, kmin], axis=1), (2 * SUBLANES, 2 * LANES))
    refs = []
    for g in range(GQA_KV_HEADS):
        qt = q_slab(g)
        zero = jnp.zeros_like(qt)
        refs.append(_dot(env, jnp.concatenate([jnp.maximum(qt, zero), jnp.minimum(qt, zero)], axis=0))[0:1, :])
    acc_ref[...] = jnp.zeros_like(acc_ref)

    def fast_body(c, carry):
        kc = k_ref[c]
        for g in range(GQA_KV_HEADS):
            p = jnp.exp2(_dot(kc, q_slab(g)) - refs[g]).astype(BF16)
            vt = vt_ref[c, g * PV_ROWS:(g + 1) * PV_ROWS, :]
            for j in range(nj):
                acc_ref[g, j] += _dot(vt, p[:, j * tq:(j + 1) * tq])
        return carry

    lax.fori_loop(0, n_chunks, fast_body, 0)
    low = acc_ref[0, 0][HEAD_DIM:HEAD_DIM + 1]
    for g in range(GQA_KV_HEADS):
        for j in range(nj):
            low = jnp.minimum(low, acc_ref[g, j][HEAD_DIM:HEAD_DIM + 1])
    fast_ok = jnp.min(low) >= FAST_PASS_MIN_DENOM

    @pl.when(fast_ok)
    def _():
        write_out()

    @pl.when(jnp.logical_not(fast_ok))
    def _():
        acc_ref[...] = jnp.zeros_like(acc_ref)

        def body(c, ms):
            kc = k_ref[c]
            new_ms = []
            for g in range(GQA_KV_HEADS):
                s = _dot(kc, q_slab(g))
                vt = vt_ref[c, g * PV_ROWS:(g + 1) * PV_ROWS, :]
                m_new = jnp.maximum(ms[g], jnp.max(s, axis=0, keepdims=True))
                alpha = jnp.exp2(ms[g] - m_new)
                p = jnp.exp2(s - m_new).astype(BF16)
                for j in range(nj):
                    sl = slice(j * tq, (j + 1) * tq)
                    acc_ref[g, j] = alpha[:, sl] * acc_ref[g, j] + _dot(vt, p[:, sl])
                new_ms.append(m_new)
            return tuple(new_ms)

        m0 = jnp.full((1, nj * tq), -jnp.inf, F32)
        lax.fori_loop(0, n_chunks, body, (m0,) * GQA_KV_HEADS)
        write_out()


def _gqa(qt, k4, vt4, q_start, n_q):
    b = qt.shape[0]
    _, n_chunks, kv_chunk, _ = k4.shape
    tq = next(w for w in GQA_Q_TILES if n_q % w == 0 and q_start % w == 0)
    q0 = q_start // tq
    return pl.pallas_call(
        _gqa_kernel,
        grid=(b, n_q // tq),
        in_specs=[
            pl.BlockSpec((None, 2 * GQA_W, tq), lambda bi, i: (bi, 0, i + q0)),
            pl.BlockSpec((None, n_chunks, kv_chunk, GQA_KV_W), lambda bi, i: (bi, 0, 0, 0)),
            pl.BlockSpec((None, n_chunks, GQA_KV_HEADS * PV_ROWS, kv_chunk), lambda bi, i: (bi, 0, 0, 0)),
        ],
        out_specs=pl.BlockSpec((None, tq, GQA_W), lambda bi, i: (bi, i, 0)),
        out_shape=jax.ShapeDtypeStruct((b, n_q, GQA_W), F32),
        scratch_shapes=[pltpu.VMEM((GQA_KV_HEADS, GQA_W // LANES, PV_ROWS, tq), F32)],
        compiler_params=_cparams("parallel", "parallel"),
        name="gqa",
    )(qt, k4, vt4)


def _natten_kernel(q_ref, k_ref, v_ref, bias_ref, o_ref, *, n_ctx, rows):
    i = pl.program_id(1)
    nq = GRID_W
    rows_per_step = q_ref.shape[0] // nq
    win = NA_WIN_ROWS * GRID_W
    is_ctx = i * rows_per_step < n_ctx // nq
    n_pairs = NA_W // LANES
    qs, kws, vws, kcs, vcs, bs = [], [], [], [], [], []
    for rr in range(rows_per_step):
        r = i * rows_per_step + rr - n_ctx // nq
        r_start = jnp.clip(r - NA_WIN_ROWS // 2, 0, rows - NA_WIN_ROWS)
        off = jnp.where(is_ctx, NA_WIN_ROWS, r_start - r + NA_WIN_ROWS - 1)
        koff = pl.multiple_of(n_ctx + r_start * GRID_W, GRID_W)
        for pair in range(n_pairs):
            sl = slice(pair * LANES, (pair + 1) * LANES)
            kw, vw = k_ref[pl.ds(koff, win), sl], v_ref[pl.ds(koff, win), sl]
            kc, vc = k_ref[0:n_ctx, sl], v_ref[0:n_ctx, sl]
            for half in range(2):
                hd = 2 * pair + half
                qs.append(q_ref[rr * nq:(rr + 1) * nq, hd * LANES:(hd + 1) * LANES])
                bs.append(bias_ref[hd, off])
                kws.append(kw)
                vws.append(vw)
                kcs.append(kc)
                vcs.append(vc)
    q = jnp.stack(qs, axis=0)
    s_w = _bdot_nt(q, jnp.stack(kws, axis=0)) + jnp.stack(bs, axis=0)
    s_c = _bdot_nt(q, jnp.stack(kcs, axis=0))
    m = jnp.maximum(jnp.max(s_w, axis=-1, keepdims=True), jnp.max(s_c, axis=-1, keepdims=True))
    p_w = jnp.exp(s_w - m)
    p_c = jnp.exp(s_c - m)
    l = jnp.sum(p_w, axis=-1, keepdims=True) + jnp.sum(p_c, axis=-1, keepdims=True)
    o = (_bdot(p_w.astype(BF16), jnp.stack(vws, axis=0)) + _bdot(p_c.astype(BF16), jnp.stack(vcs, axis=0))) / l
    lane = lax.broadcasted_iota(jnp.int32, (nq, LANES), 1)
    for rr in range(rows_per_step):
        for pair in range(n_pairs):
            e = (rr * n_pairs + pair) * 2
            o_ref[rr * nq:(rr + 1) * nq, pair * LANES:(pair + 1) * LANES] = jnp.where(lane < HEAD_DIM, o[e], o[e + 1])


def _natten(nq4, nk, nv, bias, n_ctx):
    b, t, _ = nk.shape
    rows = (t - n_ctx) // GRID_W
    assert rows >= NA_WIN_ROWS and n_ctx % TOKEN_TILE == 0 and TOKEN_TILE % GRID_W == 0
    kv = pl.BlockSpec((None, t, NA_W), lambda bi, i: (bi, 0, 0))
    return pl.pallas_call(
        functools.partial(_natten_kernel, n_ctx=n_ctx, rows=rows),
        grid=(b, t // TOKEN_TILE),
        in_specs=[
            pl.BlockSpec((None, TOKEN_TILE, NA_HEADS * LANES), lambda bi, i: (bi, i, 0)),
            kv, kv,
            pl.BlockSpec(bias.shape, lambda bi, i: (0, 0, 0, 0)),
        ],
        out_specs=pl.BlockSpec((None, TOKEN_TILE, NA_W), lambda bi, i: (bi, i, 0)),
        out_shape=jax.ShapeDtypeStruct((b, t, NA_W), F32),
        compiler_params=_cparams("parallel", "parallel"),
        name="natten",
    )(nq4, nk, nv, bias)


def _natten_bias(rpb):
    cols = np.arange(GRID_W)
    c_start = np.clip(cols - NA_WIN_COLS // 2, 0, GRID_W - NA_WIN_COLS)
    key = np.arange(GRID_W)
    valid = (key[None, :] >= c_start[:, None]) & (key[None, :] < c_start[:, None] + NA_WIN_COLS)
    dc = key[None, :] - cols[:, None] + NA_WIN_COLS - 1
    place = (valid[:, :, None] & (dc[:, :, None] == np.arange(2 * NA_WIN_COLS - 1))).astype(np.float32)
    mask = np.where(valid, 0.0, MASK_BIAS).astype(np.float32)
    rows = jnp.stack([rpb[:, o:o + NA_WIN_ROWS, :] for o in range(NA_WIN_ROWS)], axis=1)
    dense = jnp.einsum("hopd,cxd->hocpx", rows, jnp.asarray(place), precision=HI) + mask[None, None, :, None, :]
    dense = dense.reshape(NA_HEADS, NA_WIN_ROWS, GRID_W, NA_WIN_ROWS * GRID_W)
    return jnp.concatenate([dense, jnp.full_like(dense[:, :1], MASK_BIAS)], axis=1).astype(F32)


def _out_kernel(x_ref, yf_ref, yb_ref, bonus_ref, yac_ref, yal_ref, yn_ref, g_ref, mod_ref, w_ref,
                gng_ref, gnb_ref, lng_ref, lnb_ref, ones_ref, o_ref, *, alpha, n_ctx_tiles, skip_tiles):
    is_ctx = pl.program_id(1) + skip_tiles < n_ctx_tiles
    ya = jnp.where(is_ctx, yac_ref[...], yal_ref[...])
    ones = ones_ref[...]
    ys = yf_ref[...] + yb_ref[...]
    mu = _head_sums(ys, ones) * (1.0 / HEAD_DIM)
    dv = ys - mu
    var = _head_sums(dv * dv, ones) * (1.0 / HEAD_DIM)
    yr = dv * lax.rsqrt(var + GN_EPS) * gng_ref[...] + gnb_ref[...] + bonus_ref[...]
    g = g_ref[...].astype(F32)
    parts = (
        yr * _silu(g[:, 0:RWKV_W]),
        ya * _silu(g[:, RWKV_W:RWKV_W + GQA_W]),
        yn_ref[...] * _silu(g[:, RWKV_W + GQA_W:]),
    )
    mixed = jnp.concatenate([p.astype(BF16) for p in parts], axis=1)
    y = _dot(mixed, w_ref[...])
    xn = alpha * x_ref[...] + mod_ref[2:3, :] * y
    mean = jnp.mean(xn, axis=-1, keepdims=True)
    xc = xn - mean
    var = jnp.mean(xc * xc, axis=-1, keepdims=True)
    o_ref[...] = xc * lax.rsqrt(var + LN_EPS) * lng_ref[...] + lnb_ref[...]


def _output(xc, yf, yb, bonus, ya_ctx, ya_lat, yn, gates, modsel, w_out, gn_g, gn_b, ln_g, ln_b, ones_heads, alpha,
            n_ctx_tiles, skip_tiles):
    b, t, d = xc.shape
    nt = t // TOKEN_TILE - skip_tiles
    tok = lambda width: pl.BlockSpec((None, TOKEN_TILE, width), lambda bi, i: (bi, i + skip_tiles, 0))
    full = lambda shape: pl.BlockSpec(shape, lambda bi, i: (0,) * len(shape))
    ctx_tile = lambda bi, i: (bi, jnp.minimum(i + skip_tiles, n_ctx_tiles - 1), 0)
    lat_tile = lambda bi, i: (bi, jnp.maximum(i + skip_tiles - n_ctx_tiles, 0), 0)
    return pl.pallas_call(
        functools.partial(_out_kernel, alpha=alpha, n_ctx_tiles=n_ctx_tiles, skip_tiles=skip_tiles),
        grid=(b, nt),
        in_specs=[
            tok(d), tok(RWKV_W), tok(RWKV_W), tok(RWKV_W),
            pl.BlockSpec((None, TOKEN_TILE, GQA_W), ctx_tile), pl.BlockSpec((None, TOKEN_TILE, GQA_W), lat_tile),
            tok(NA_W), tok(RWKV_W + GQA_W + NA_W),
            pl.BlockSpec((None, None, 3, d), lambda bi, i: (bi, jnp.where(i + skip_tiles < n_ctx_tiles, 0, 1), 0, 0)),
            full(w_out.shape), full((1, RWKV_W)), full((1, RWKV_W)), full((1, d)), full((1, d)), full((2 * RWKV_W, RWKV_W)),
        ],
        out_specs=pl.BlockSpec((None, TOKEN_TILE, d), lambda bi, i: (bi, i, 0)),
        out_shape=jax.ShapeDtypeStruct((b, nt * TOKEN_TILE, d), F32),
        compiler_params=_cparams("parallel", "parallel"),
        name="out",
    )(xc, yf, yb, bonus, ya_ctx, ya_lat, yn, gates, modsel, w_out, gn_g, gn_b, ln_g, ln_b, ones_heads)


def _rope_tables(n_ctx, n_lat):
    t = jnp.arange(n_lat, dtype=jnp.int32)
    row = (t // GRID_W).astype(F32)
    col = (t % GRID_W).astype(F32)
    n_freq = HEAD_DIM // 4
    inv_freq = ROPE_THETA ** (-jnp.arange(n_freq, dtype=F32) / n_freq)
    ang_r = row[:, None] * inv_freq
    ang_c = col[:, None] * inv_freq
    ang = jnp.concatenate([ang_r, ang_r, ang_c, ang_c], axis=-1)
    cos = jnp.concatenate([jnp.ones((n_ctx, HEAD_DIM), F32), jnp.cos(ang)], axis=0)
    sin = jnp.concatenate([jnp.zeros((n_ctx, HEAD_DIM), F32), jnp.sin(ang)], axis=0)
    first = (np.arange(HEAD_DIM) % (HEAD_DIM // 2)) < HEAD_DIM // 4
    sin_a = jnp.where(first, -sin, 0.0)
    sin_b = jnp.where(first, 0.0, sin)
    pair = lambda z: jnp.concatenate([z, z], axis=-1)
    return pair(cos), pair(sin_a), pair(sin_b)


def _block_ones(n_heads):
    blk = np.kron(np.eye(n_heads, dtype=np.float32), np.ones((HEAD_DIM, HEAD_DIM), np.float32))
    return jnp.asarray(blk)


def kernel(x, c, ctx, c_ctx, w_mod, b_mod, w_in, w_out, rwkv_conv, decay_w0, decay_w2, iclr_a0, iclr_a2, rwkv_k_k, rwkv_k_a, rwkv_r_k, rwkv_gn_g, rwkv_gn_b, gqa_q_norm, gqa_k_norm, na_rpb, ln_g, ln_b):
    b, n_lat, d = x.shape
    n_ctx = ctx.shape[1]
    depth = w_mod.shape[0]
    alpha = float((2 * depth) ** 0.25)
    n_ctx_tiles = n_ctx // TOKEN_TILE
    assert n_ctx % TOKEN_TILE == 0 and n_lat % TOKEN_TILE == 0 and n_lat % GRID_W == 0

    n_rows = -(-(b + 1) // SUBLANES) * SUBLANES
    cvec = jnp.zeros((n_rows, d), F32).at[:b].set(c).at[b].set(c_ctx)
    mod = _modulation(cvec, w_mod, b_mod).reshape(depth, n_rows, 3, d)
    mod_ctx = jnp.broadcast_to(mod[:, b][:, None], (depth, b, 3, d))
    modsel = jnp.stack([mod_ctx, mod[:, :b]], axis=2)

    sizes = (RWKV_CONV_W, RWKV_W, GQA_W, GQA_KV_W, GQA_KV_W, GQA_W, NA_W, NA_W, NA_W, NA_W)
    offs = np.concatenate([[0], np.cumsum(sizes)])
    seg = lambda n: w_in[:, :, offs[n]:offs[n + 1]]

    def pair_heads(z, axis):
        shp = z.shape[:axis] + (GQA_KV_HEADS, GQA_Q_HEADS // GQA_KV_HEADS, HEAD_DIM) + z.shape[axis + 1:]
        return jnp.swapaxes(z.reshape(shp), axis, axis + 1).reshape(z.shape)

    w_in_p = jnp.concatenate([seg(0), pair_heads(seg(2), 2), seg(3), seg(4), seg(6), seg(7), seg(8),
                              seg(1), pair_heads(seg(5), 2), seg(9)], axis=2).astype(BF16)
    w_out_p = jnp.concatenate([w_out[:, :RWKV_W], pair_heads(w_out[:, RWKV_W:RWKV_W + GQA_W], 1),
                               w_out[:, RWKV_W + GQA_W:]], axis=1).astype(BF16)

    w = RWKV_W
    zero = jnp.zeros((depth, LOW_RANK, w), F32)
    blocks = (decay_w2[:, 0], decay_w2[:, 1], iclr_a2[:, 0], iclr_a2[:, 1])
    wlr = jnp.concatenate([jnp.concatenate([blk if cc == rr else zero for cc in range(4)], axis=2)
                           for rr, blk in enumerate(blocks)], axis=1)
    wlr_hi = wlr.astype(BF16)
    wlr_lo = (wlr - wlr_hi.astype(F32)).astype(BF16)
    wlr3 = jnp.concatenate([wlr_hi, wlr_hi, wlr_lo], axis=1)
    b0 = jnp.concatenate([decay_w0[:, 0], decay_w0[:, 1], iclr_a0[:, 0], iclr_a0[:, 1]], axis=-1)[:, None, :]

    cos, sin_a, sin_b = _rope_tables(n_ctx, n_lat)
    ones_heads = jnp.tile(_block_ones(RWKV_HEADS), (2, 1)).astype(BF16)
    ones_pair = jnp.tile(_block_ones(2), (2, 1)).astype(BF16)
    pair = lambda z: jnp.concatenate([z, z], axis=-1)[:, None, :]
    qn, kn = pair(gqa_q_norm), pair(gqa_k_norm)

    xc = jnp.concatenate([ctx, x], axis=1)
    t = n_ctx + n_lat
    kv_chunk = next(ch for ch in KV_CHUNKS if t % ch == 0)
    for l in range(depth):
        ru, aq, ak, av, nq4, nk, nv, gates = _projection(xc, modsel[l], w_in_p[l], n_ctx_tiles)
        prep = _rwkv_prep(ru, rwkv_conv[l], wlr3[l], b0[l], rwkv_k_k[l][None], rwkv_k_a[l][None],
                          rwkv_r_k[l].reshape(1, w), ones_heads, n_ctx_tiles)
        yf, yb = _rwkv_scan(prep, n_ctx)
        last = l == depth - 1
        qt, kr, vt4 = _qk_prep(aq, ak, av, cos, sin_a, sin_b, qn[l], kn[l], ones_pair, kv_chunk, n_ctx_tiles)
        k4 = kr.reshape(b, t // kv_chunk, kv_chunk, GQA_KV_W)
        ya = _gqa(qt, k4, vt4, 0, n_lat)
        if last:
            ya_c = ya
        else:
            ya_c = _gqa(qt, kr[:, :n_ctx].reshape(b, 1, n_ctx, GQA_KV_W), vt4[:, :1, :, :n_ctx], n_lat, n_ctx)
        yn = _natten(nq4, nk, nv, _natten_bias(na_rpb[l]), n_ctx)
        xc = _output(xc, yf, yb, prep[9], ya_c, ya, yn, gates, modsel[l], w_out_p[l], rwkv_gn_g[l][None],
                     rwkv_gn_b[l][None], ln_g[l][None], ln_b[l][None], ones_heads, alpha, n_ctx_tiles,
                     n_ctx_tiles if last else 0)
    return xc
```

```python
import functools

import numpy as np
import jax
import jax.numpy as jnp
from jax import lax
from jax.experimental import pallas as pl
from jax.experimental.pallas import tpu as pltpu

F32 = jnp.float32
BF16 = jnp.bfloat16
HI = lax.Precision.HIGHEST

HEAD_DIM = 64
RWKV_HEADS = 6
GQA_Q_HEADS = 6
GQA_KV_HEADS = 2
NA_HEADS = 4
RWKV_W = RWKV_HEADS * HEAD_DIM
GQA_W = GQA_Q_HEADS * HEAD_DIM
GQA_KV_W = GQA_KV_HEADS * HEAD_DIM
NA_W = NA_HEADS * HEAD_DIM
LOW_RANK = 32
RWKV_CONV_W = 3 * RWKV_W + 4 * LOW_RANK
GRID_W = 64
NA_WIN_ROWS = 8
NA_WIN_COLS = 16
ROPE_THETA = 10000.0
LN_EPS = 1e-5
RMS_EPS = 1e-6
GN_EPS = 64e-5
KK_EPS = 1e-12
MASK_BIAS = -1e30

LANES = 128
SUBLANES = 8
TOKEN_TILE = 256
SCAN_CHUNK = 64
SCAN_CHUNKS_PER_STEP = 4
KV_CHUNKS = (768, 512, 256)
GQA_Q_TILES = (1024, 512, 256)
PV_ROWS = HEAD_DIM + 16
FAST_PASS_MIN_DENOM = 2.0 ** -64
LOG2_E = 1.4426950408889634
VMEM_LIMIT = 56 * 1024 * 1024


def _cparams(*sem):
    return pltpu.CompilerParams(dimension_semantics=sem, vmem_limit_bytes=VMEM_LIMIT)


def _dot(a, b, precision=None):
    return jnp.dot(a, b, preferred_element_type=F32, precision=precision)


def _b(x):
    return x.astype(BF16)


def _split2(x):
    hi = x.astype(BF16)
    return hi, (x - hi.astype(F32)).astype(BF16)


def _head_sums(x, ones2):
    hi, lo = _split2(x)
    return _dot(jnp.concatenate([hi, lo], axis=1), ones2)


def _sigmoid(x):
    return 1.0 / (1.0 + jnp.exp(-x))


def _silu(x):
    return x * _sigmoid(x)


def _mod_kernel(c_ref, w_ref, b_ref, o_ref):
    o_ref[...] = _dot(_silu(c_ref[...]), w_ref[...], HI) + b_ref[...]


def _modulation(cvec, w_mod, b_mod):
    depth, d, d3 = w_mod.shape
    r = cvec.shape[0]
    nj = d3 // d
    return pl.pallas_call(
        _mod_kernel,
        grid=(depth, nj),
        in_specs=[
            pl.BlockSpec((r, d), lambda l, j: (0, 0)),
            pl.BlockSpec((None, d, d), lambda l, j: (l, 0, j)),
            pl.BlockSpec((None, 1, d), lambda l, j: (l, 0, j)),
        ],
        out_specs=pl.BlockSpec((None, r, d), lambda l, j: (l, 0, j)),
        out_shape=jax.ShapeDtypeStruct((depth, r, d3), F32),
        compiler_params=_cparams("parallel", "parallel"),
        name="modulation",
    )(cvec, w_mod, b_mod.reshape(depth, 1, d3))


_PROJ_SLABS = (
    ("ru", RWKV_CONV_W, F32),
    ("aq", GQA_W, F32),
    ("ak", GQA_KV_W, F32),
    ("av", GQA_KV_W, BF16),
    ("nq", NA_W, None),
    ("nk", NA_W, BF16),
    ("nv", NA_W, BF16),
    ("gates", RWKV_W + GQA_W + NA_W, BF16),
)


def _proj_kernel(x_ref, mod_ref, w_ref, ru_ref, aq_ref, ak_ref, av_ref, nq_ref, nk_ref, nv_ref, g_ref):
    shift = mod_ref[0:1, :]
    scale = mod_ref[1:2, :]
    h = (x_ref[...] * (1.0 + scale) + shift).astype(BF16)
    outs = {"ru": ru_ref, "aq": aq_ref, "ak": ak_ref, "av": av_ref, "nk": nk_ref, "nv": nv_ref, "gates": g_ref}
    off = 0
    for name, width, _ in _PROJ_SLABS:
        u = _dot(h, w_ref[:, off:off + width])
        if name == "nq":
            u = u * (HEAD_DIM ** -0.5)
            lane = lax.broadcasted_iota(jnp.int32, (u.shape[0], LANES), 1)
            for hd in range(NA_HEADS):
                pair = u[:, (hd // 2) * LANES:(hd // 2 + 1) * LANES]
                keep = (lane < HEAD_DIM) if hd % 2 == 0 else (lane >= HEAD_DIM)
                nq_ref[:, hd * LANES:(hd + 1) * LANES] = jnp.where(keep, pair, 0.0).astype(BF16)
        else:
            outs[name][...] = u.astype(outs[name].dtype)
        off += width


def _projection(xc, modsel, w_perm, n_ctx_tiles):
    b, t, d = xc.shape
    nt = t // TOKEN_TILE
    n_in = w_perm.shape[1]
    shapes, specs = [], []
    for name, width, dt in _PROJ_SLABS:
        if name == "nq":
            width, dt = NA_HEADS * LANES, BF16
        shapes.append(jax.ShapeDtypeStruct((b, t, width), dt))
        specs.append(pl.BlockSpec((None, TOKEN_TILE, width), lambda bi, i: (bi, i, 0)))
    return pl.pallas_call(
        _proj_kernel,
        grid=(b, nt),
        in_specs=[
            pl.BlockSpec((None, TOKEN_TILE, d), lambda bi, i: (bi, i, 0)),
            pl.BlockSpec((None, None, 3, d), lambda bi, i: (bi, jnp.where(i < n_ctx_tiles, 0, 1), 0, 0)),
            pl.BlockSpec((d, n_in), lambda bi, i: (0, 0)),
        ],
        out_specs=specs,
        out_shape=shapes,
        compiler_params=_cparams("parallel", "parallel"),
        name="proj",
    )(xc, modsel, w_perm)


def _rwkv_prep_kernel(x_ref, xp_ref, xn_ref, cw_ref, wlr_ref, b0_ref, kk_ref, ka_ref, rk_ref, ones_ref,
                      r_o, v_o, kk_o, lw0_o, kd0_o, bd0_o, lw1_o, kd1_o, bd1_o, bonus_o, *, n_ctx_tiles):
    i = pl.program_id(1)
    nt = pl.num_programs(1)
    tm = x_ref.shape[0]
    prev_ok = jnp.logical_and(i != 0, i != n_ctx_tiles)
    next_ok = jnp.logical_and(i != n_ctx_tiles - 1, i != nt - 1)
    x = x_ref[...]
    row = lax.broadcasted_iota(jnp.int32, x.shape, 0)
    halo_prev = jnp.where(prev_ok, xp_ref[SUBLANES - 1:SUBLANES, :], 0.0)
    halo_next = jnp.where(next_ok, xn_ref[0:1, :], 0.0)
    x_prev = jnp.where(row == 0, halo_prev, pltpu.roll(x, 1, 0))
    x_next = jnp.where(row == tm - 1, halo_next, pltpu.roll(x, tm - 1, 0))
    u = cw_ref[0:1, :] * x_prev + cw_ref[1:2, :] * x + cw_ref[2:3, :] * x_next

    w = RWKV_W
    r, k, v = u[:, 0:w], u[:, w:2 * w], u[:, 2 * w:3 * w]
    lr = u[:, 3 * w:3 * w + LANES]
    lane = lax.broadcasted_iota(jnp.int32, lr.shape, 1)
    z = jnp.where(lane < 2 * LOW_RANK, jnp.tanh(lr), lr)
    z_hi, z_lo = _split2(z)
    pre = _dot(jnp.concatenate([z_hi, z_lo, z_hi], axis=1), wlr_ref[...]) + b0_ref[...]

    ones = ones_ref[...]
    kk0 = k * kk_ref[...]
    ss = _head_sums(kk0 * kk0, ones)
    kk = kk0 * lax.rsqrt(jnp.maximum(ss, KK_EPS))
    r_o[...] = r
    v_o[...] = v
    kk_o[...] = kk
    kd_sum = None
    for d, (lw_o, kd_o, bd_o) in enumerate(((lw0_o, kd0_o, bd0_o), (lw1_o, kd1_o, bd1_o))):
        xd = -pre[:, d * w:(d + 1) * w]
        softplus = jnp.maximum(xd, 0.0) + jnp.log(1.0 + jnp.exp(-jnp.abs(xd)))
        lw_o[...] = -jnp.exp(-softplus - 0.5)
        iclr = _sigmoid(pre[:, (2 + d) * w:(3 + d) * w])
        kd = k * (1.0 + (iclr - 1.0) * ka_ref[...])
        kd_o[...] = kd
        bd_o[...] = kk * iclr
        kd_sum = kd if kd_sum is None else kd_sum + kd
    bonus_o[...] = _head_sums(r * kd_sum * rk_ref[...], ones) * v


def _rwkv_prep(ru, conv_w, wlr, b0, k_k, k_a, r_k, ones_heads, n_ctx_tiles):
    b, t, cw = ru.shape
    nt = t // TOKEN_TILE
    hb = TOKEN_TILE // SUBLANES
    w = RWKV_W
    full = lambda shape: pl.BlockSpec(shape, lambda bi, i: (0,) * len(shape))
    out_spec = pl.BlockSpec((None, TOKEN_TILE, w), lambda bi, i: (bi, i, 0))
    return pl.pallas_call(
        functools.partial(_rwkv_prep_kernel, n_ctx_tiles=n_ctx_tiles),
        grid=(b, nt),
        in_specs=[
            pl.BlockSpec((None, TOKEN_TILE, cw), lambda bi, i: (bi, i, 0)),
            pl.BlockSpec((None, SUBLANES, cw), lambda bi, i: (bi, jnp.maximum(i * hb - 1, 0), 0)),
            pl.BlockSpec((None, SUBLANES, cw), lambda bi, i: (bi, jnp.minimum((i + 1) * hb, t // SUBLANES - 1), 0)),
            full((3, cw)), full((3 * LANES, 4 * w)), full((1, 4 * w)),
            full((1, w)), full((1, w)), full((1, w)), full((2 * w, w)),
        ],
        out_specs=[out_spec] * 10,
        out_shape=[jax.ShapeDtypeStruct((b, t, w), F32)] * 10,
        compiler_params=_cparams("parallel", "parallel"),
        name="rwkv_prep",
    )(ru, ru, ru, conv_w, wlr, b0, k_k, k_a, r_k, ones_heads)


def _bdot(a, b):
    return lax.dot_general(a, b, (((2,), (1,)), ((0,), (0,))), preferred_element_type=F32)


def _bdot_nt(a, b):
    return lax.dot_general(a, b, (((2,), (2,)), ((0,), (0,))), preferred_element_type=F32)


def _bdot_tn(a, b):
    return lax.dot_general(a, b, (((1,), (1,)), ((0,), (0,))), preferred_element_type=F32)


def _chunk_local(r, k, v, kap, bet, lw, consts):
    tri, rev, strict, incl, head_rows, eye2, eye_l = consts
    c = r.shape[1]
    l1 = lw.astype(BF16)
    l2 = (lw - l1.astype(F32)).astype(BF16)
    l3 = (lw - l1.astype(F32) - l2.astype(F32)).astype(BF16)
    lcs = _bdot(tri, jnp.concatenate([l1, l2, l3], axis=2))
    lc = lcs[:, :, 0:LANES] + lcs[:, :, LANES:2 * LANES] + lcs[:, :, 2 * LANES:3 * LANES]
    ltot = jnp.where(rev, lc[:, 0:1, :], lc[:, c - 1:c, :])
    e_in = jnp.exp(lc)
    e_inv = jnp.exp(-lc)
    e_hat = jnp.exp(ltot - lc)

    def stack(x):
        return jnp.where(head_rows, jnp.concatenate([x, x], axis=1), 0.0)

    r_s = stack(r * e_in)
    kap_s = stack(kap * jnp.exp(lc - lw))
    k_s = stack(k * e_inv)
    b_s = stack(bet * e_inv)
    kh_s = stack(k * e_hat)
    bh_s = stack(bet * e_hat)
    v_s = stack(v)

    m = _bdot_nt(_b(jnp.concatenate([kap_s, r_s], axis=1)), _b(jnp.concatenate([b_s, k_s], axis=1)))
    c2 = 2 * c
    a = jnp.where(strict, m[:, 0:c2, 0:c2], 0.0)
    bm = jnp.where(strict, m[:, 0:c2, c2:2 * c2], 0.0)
    mb = jnp.where(incl, m[:, c2:2 * c2, 0:c2], 0.0)
    mk = jnp.where(incl, m[:, c2:2 * c2, c2:2 * c2], 0.0)

    t_inv = eye2 - a
    pw = a
    for _ in range(int(np.log2(c)) - 1):
        pw_b = _b(_bdot(_b(pw), _b(pw)))
        pw = pw_b
        t_inv = t_inv + _bdot(_b(t_inv), pw_b)

    v_b = _b(v_s)
    bv = _bdot(_b(bm), v_b)
    x_b = _b(_bdot(_b(t_inv), _b(jnp.concatenate([kap_s, bv], axis=2))))
    z = _bdot(_b(mb), x_b)
    rp = r_s - z[:, :, 0:LANES]
    y0 = _bdot(_b(mk), v_b) - z[:, :, LANES:2 * LANES]
    bhx = _bdot_tn(_b(bh_s), x_b)
    g = jnp.where(eye_l, jnp.exp(ltot), 0.0) - bhx[:, :, 0:LANES]
    hadd = _bdot_tn(_b(kh_s), v_b) - bhx[:, :, LANES:2 * LANES]
    return rp, y0, g, hadd


def _rwkv_scan_kernel(rf, vf, kkf, lwf, kdf, bdf, rb, vb, kkb, lwb, kdb, bdb, yf_ref, yb_ref, h_ref):
    i = pl.program_id(1)

    @pl.when(i == 0)
    def _():
        h_ref[...] = jnp.zeros_like(h_ref)

    c = SCAN_CHUNK
    c2 = 2 * c
    n_slots = rf.shape[0] // c
    n_pairs = RWKV_W // LANES
    n_dir = 2 * n_pairs
    n = n_slots * n_dir
    iota = lambda shape, dim: lax.broadcasted_iota(jnp.int32, shape, dim)
    is_rev = lambda shape: lax.rem(iota(shape, 0), n_dir) >= n_pairs
    rev = is_rev((n, 1, 1))
    sign = lambda shape: jnp.where(is_rev(shape), 1, -1)
    tri = jnp.where((iota((n, c, c), 2) - iota((n, c, c), 1)) * sign((n, c, c)) >= 0, 1.0, 0.0).astype(BF16)
    row2, col2 = iota((n, c2, c2), 1), iota((n, c2, c2), 2)
    same = (row2 < c) == (col2 < c)
    rt = jnp.where(row2 < c, row2, row2 - c)
    ct = jnp.where(col2 < c, col2, col2 - c)
    before = (ct - rt) * sign((n, c2, c2))
    strict = jnp.logical_and(same, before > 0)
    incl = jnp.logical_and(same, before >= 0)
    head_rows = (iota((n, c2, LANES), 1) < c) == (iota((n, c2, LANES), 2) < HEAD_DIM)
    eye2 = jnp.where(row2 == col2, 1.0, 0.0)
    eye_l = iota((n, LANES, LANES), 1) == iota((n, LANES, LANES), 2)
    consts = (tri, rev, strict, incl, head_rows, eye2, eye_l)

    def rows(s, reverse):
        s = n_slots - 1 - s if reverse else s
        return slice(s * c, (s + 1) * c)

    def chains(fref, bref):
        return jnp.stack([ref[rows(s, ref is bref), p * LANES:(p + 1) * LANES]
                          for s in range(n_slots) for ref in (fref, bref) for p in range(n_pairs)], axis=0)

    rp, y0, g, hadd = _chunk_local(chains(rf, rb), chains(kdf, kdb), chains(vf, vb), chains(kkf, kkb),
                                   chains(bdf, bdb), chains(lwf, lwb), consts)
    h = h_ref[...]
    for s in range(n_slots):
        cs = slice(s * n_dir, (s + 1) * n_dir)
        h_b = _b(h)
        ys = _bdot(_b(rp[cs]), h_b) + y0[cs]
        h = _bdot(_b(g[cs]), h_b) + hadd[cs]
        y = ys[:, 0:c, :] + ys[:, c:c2, :]
        for p in range(n_pairs):
            yf_ref[rows(s, False), p * LANES:(p + 1) * LANES] = y[p]
            yb_ref[rows(s, True), p * LANES:(p + 1) * LANES] = y[n_pairs + p]
    h_ref[...] = h


def _rwkv_scan(prep, n_ctx):
    r, v, kk, lw0, kd0, bd0, lw1, kd1, bd1, _ = prep
    b, t, w = r.shape
    blk = SCAN_CHUNK * SCAN_CHUNKS_PER_STEP
    assert n_ctx % blk == 0 and t % blk == 0
    nb, ncb = t // blk, n_ctx // blk

    def bwd_block(i):
        return jnp.where(i < ncb, ncb - 1 - i, nb - 1 + ncb - i)

    fspec = pl.BlockSpec((None, blk, w), lambda bi, i: (bi, i, 0))
    bspec = pl.BlockSpec((None, blk, w), lambda bi, i: (bi, bwd_block(i), 0))
    return pl.pallas_call(
        _rwkv_scan_kernel,
        grid=(b, nb),
        in_specs=[fspec] * 6 + [bspec] * 6,
        out_specs=[fspec, bspec],
        out_shape=[jax.ShapeDtypeStruct((b, t, w), F32)] * 2,
        scratch_shapes=[pltpu.VMEM((2 * (w // LANES), LANES, LANES), F32)],
        compiler_params=_cparams("parallel", "arbitrary"),
        name="rwkv_scan",
    )(r, v, kk, lw0, kd0, bd0, r, v, kk, lw1, kd1, bd1)


def _qk_prep_kernel(aq_ref, ak_ref, av_ref, cos_ref, sa_ref, sb_ref, qn_ref, kn_ref, ones_ref, qt_o, k_o, vt_o):
    cos, sin_a, sin_b = cos_ref[...], sa_ref[...], sb_ref[...]
    ones = ones_ref[...]
    tm = cos.shape[0]
    row = lax.broadcasted_iota(jnp.int32, (LANES, tm), 0)

    def norm_rope(x, g):
        ms = _head_sums(x * x, ones) * (1.0 / HEAD_DIM)
        xn = x * lax.rsqrt(ms + RMS_EPS) * g
        return (xn * cos + pltpu.roll(xn, LANES - HEAD_DIM // 4, 1) * sin_a
                + pltpu.roll(xn, HEAD_DIM // 4, 1) * sin_b)

    for j in range(GQA_W // LANES):
        q = norm_rope(aq_ref[:, j * LANES:(j + 1) * LANES], qn_ref[...]) * (HEAD_DIM ** -0.5 * LOG2_E)
        qt = q.T
        qt_o[(2 * j) * LANES:(2 * j + 1) * LANES, :] = jnp.where(row < HEAD_DIM, qt, 0.0).astype(BF16)
        qt_o[(2 * j + 1) * LANES:(2 * j + 2) * LANES, :] = jnp.where(row >= HEAD_DIM, qt, 0.0).astype(BF16)
    k_o[...] = norm_rope(ak_ref[...], kn_ref[...]).astype(BF16)
    vt = av_ref[...].astype(F32).T
    one = jnp.ones((PV_ROWS - HEAD_DIM, tm), F32)
    vt_o[...] = jnp.concatenate([vt[0:HEAD_DIM], one, vt[HEAD_DIM:2 * HEAD_DIM], one], axis=0).astype(BF16)


def _qk_prep(aq, ak, av, cos, sin_a, sin_b, qn, kn, ones_pair, kv_chunk, n_ctx_tiles):
    b, t, _ = aq.shape
    nt = t // TOKEN_TILE
    per = kv_chunk // TOKEN_TILE
    tok = lambda width: pl.BlockSpec((None, TOKEN_TILE, width), lambda bi, i: (bi, i, 0))
    tab = pl.BlockSpec((TOKEN_TILE, LANES), lambda bi, i: (i, 0))
    full = lambda shape: pl.BlockSpec(shape, lambda bi, i: (0,) * len(shape))
    return pl.pallas_call(
        _qk_prep_kernel,
        grid=(b, nt),
        in_specs=[tok(GQA_W), tok(GQA_KV_W), tok(GQA_KV_W), tab, tab, tab,
                  full((1, LANES)), full((1, LANES)), full((2 * LANES, LANES))],
        out_specs=[
            pl.BlockSpec((None, 2 * GQA_W, TOKEN_TILE),
                         lambda bi, i: (bi, 0, jnp.where(i < n_ctx_tiles, i + nt - n_ctx_tiles, i - n_ctx_tiles))),
            tok(GQA_KV_W),
            pl.BlockSpec((None, None, GQA_KV_HEADS * PV_ROWS, TOKEN_TILE), lambda bi, i: (bi, i // per, 0, i % per)),
        ],
        out_shape=[
            jax.ShapeDtypeStruct((b, 2 * GQA_W, t), BF16),
            jax.ShapeDtypeStruct((b, t, GQA_KV_W), BF16),
            jax.ShapeDtypeStruct((b, t // kv_chunk, GQA_KV_HEADS * PV_ROWS, kv_chunk), BF16),
        ],
        compiler_params=_cparams("parallel", "parallel"),
        name="qk_prep",
    )(aq, ak, av, cos, sin_a, sin_b, qn, kn, ones_pair)


def _gqa_kernel(qt_ref, k_ref, vt_ref, o_ref, acc_ref):
    n_chunks = k_ref.shape[0]
    tq = qt_ref.shape[1]
    nj = GQA_W // LANES

    def q_slab(g):
        return jnp.concatenate([qt_ref[(2 * j + g) * LANES:(2 * j + g + 1) * LANES, :] for j in range(nj)], axis=1)

    def write_out():
        for j in range(nj):
            outs = []
            for g in range(GQA_KV_HEADS):
                acc = acc_ref[g, j]
                outs.append(acc[0:HEAD_DIM] / acc[HEAD_DIM:HEAD_DIM + 1])
            o_ref[:, j * LANES:(j + 1) * LANES] = jnp.concatenate(outs, axis=0).T

    def env_body(c, carry):
        kc = k_ref[c]
        return (jnp.maximum(carry[0], jnp.max(kc, axis=0, keepdims=True)),
                jnp.minimum(carry[1], jnp.min(kc, axis=0, keepdims=True)))

    k0 = k_ref[0]
    kmax, kmin = lax.fori_loop(1, n_chunks, env_body,
                               (jnp.max(k0, axis=0, keepdims=True), jnp.min(k0, axis=0, keepdims=True)))
    env = jnp.broadcast_to(jnp.concatenate([kmax, kmin], axis=1), (2 * SUBLANES, 2 * LANES))
    refs = []
    for g in range(GQA_KV_HEADS):
        qt = q_slab(g)
        zero = jnp.zeros_like(qt)
        refs.append(_dot(env, jnp.concatenate([jnp.maximum(qt, zero), jnp.minimum(qt, zero)], axis=0))[0:1, :])
    acc_ref[...] = jnp.zeros_like(acc_ref)

    def fast_body(c, carry):
        kc = k_ref[c]
        for g in range(GQA_KV_HEADS):
            p = jnp.exp2(_dot(kc, q_slab(g)) - refs[g]).astype(BF16)
            vt = vt_ref[c, g * PV_ROWS:(g + 1) * PV_ROWS, :]
            for j in range(nj):
                acc_ref[g, j] += _dot(vt, p[:, j * tq:(j + 1) * tq])
        return carry

    lax.fori_loop(0, n_chunks, fast_body, 0)
    low = acc_ref[0, 0][HEAD_DIM:HEAD_DIM + 1]
    for g in range(GQA_KV_HEADS):
        for j in range(nj):
            low = jnp.minimum(low, acc_ref[g, j][HEAD_DIM:HEAD_DIM + 1])
    fast_ok = jnp.min(low) >= FAST_PASS_MIN_DENOM

    @pl.when(fast_ok)
    def _():
        write_out()

    @pl.when(jnp.logical_not(fast_ok))
    def _():
        acc_ref[...] = jnp.zeros_like(acc_ref)

        def body(c, ms):
            kc = k_ref[c]
            new_ms = []
            for g in range(GQA_KV_HEADS):
                s = _dot(kc, q_slab(g))
                vt = vt_ref[c, g * PV_ROWS:(g + 1) * PV_ROWS, :]
                m_new = jnp.maximum(ms[g], jnp.max(s, axis=0, keepdims=True))
                alpha = jnp.exp2(ms[g] - m_new)
                p = jnp.exp2(s - m_new).astype(BF16)
                for j in range(nj):
                    sl = slice(j * tq, (j + 1) * tq)
                    acc_ref[g, j] = alpha[:, sl] * acc_ref[g, j] + _dot(vt, p[:, sl])
                new_ms.append(m_new)
            return tuple(new_ms)

        m0 = jnp.full((1, nj * tq), -jnp.inf, F32)
        lax.fori_loop(0, n_chunks, body, (m0,) * GQA_KV_HEADS)
        write_out()


def _gqa(qt, k4, vt4, q_start, n_q):
    b = qt.shape[0]
    _, n_chunks, kv_chunk, _ = k4.shape
    tq = next(w for w in GQA_Q_TILES if n_q % w == 0 and q_start % w == 0)
    q0 = q_start // tq
    return pl.pallas_call(
        _gqa_kernel,
        grid=(b, n_q // tq),
        in_specs=[
            pl.BlockSpec((None, 2 * GQA_W, tq), lambda bi, i: (bi, 0, i + q0)),
            pl.BlockSpec((None, n_chunks, kv_chunk, GQA_KV_W), lambda bi, i: (bi, 0, 0, 0)),
            pl.BlockSpec((None, n_chunks, GQA_KV_HEADS * PV_ROWS, kv_chunk), lambda bi, i: (bi, 0, 0, 0)),
        ],
        out_specs=pl.BlockSpec((None, tq, GQA_W), lambda bi, i: (bi, i, 0)),
        out_shape=jax.ShapeDtypeStruct((b, n_q, GQA_W), F32),
        scratch_shapes=[pltpu.VMEM((GQA_KV_HEADS, GQA_W // LANES, PV_ROWS, tq), F32)],
        compiler_params=_cparams("parallel", "parallel"),
        name="gqa",
    )(qt, k4, vt4)


def _natten_kernel(q_ref, k_ref, v_ref, bias_ref, o_ref, *, n_ctx, rows):
    i = pl.program_id(1)
    nq = GRID_W
    rows_per_step = q_ref.shape[0] // nq
    win = NA_WIN_ROWS * GRID_W
    is_ctx = i * rows_per_step < n_ctx // nq
    n_pairs = NA_W // LANES
    qs, kws, vws, kcs, vcs, bs = [], [], [], [], [], []
    for rr in range(rows_per_step):
        r = i * rows_per_step + rr - n_ctx // nq
        r_start = jnp.clip(r - NA_WIN_ROWS // 2, 0, rows - NA_WIN_ROWS)
        off = jnp.where(is_ctx, NA_WIN_ROWS, r_start - r + NA_WIN_ROWS - 1)
        koff = pl.multiple_of(n_ctx + r_start * GRID_W, GRID_W)
        for pair in range(n_pairs):
            sl = slice(pair * LANES, (pair + 1) * LANES)
            kw, vw = k_ref[pl.ds(koff, win), sl], v_ref[pl.ds(koff, win), sl]
            kc, vc = k_ref[0:n_ctx, sl], v_ref[0:n_ctx, sl]
            for half in range(2):
                hd = 2 * pair + half
                qs.append(q_ref[rr * nq:(rr + 1) * nq, hd * LANES:(hd + 1) * LANES])
                bs.append(bias_ref[hd, off])
                kws.append(kw)
                vws.append(vw)
                kcs.append(kc)
                vcs.append(vc)
    q = jnp.stack(qs, axis=0)
    s_w = _bdot_nt(q, jnp.stack(kws, axis=0)) + jnp.stack(bs, axis=0)
    s_c = _bdot_nt(q, jnp.stack(kcs, axis=0))
    m = jnp.maximum(jnp.max(s_w, axis=-1, keepdims=True), jnp.max(s_c, axis=-1, keepdims=True))
    p_w = jnp.exp(s_w - m)
    p_c = jnp.exp(s_c - m)
    l = jnp.sum(p_w, axis=-1, keepdims=True) + jnp.sum(p_c, axis=-1, keepdims=True)
    o = (_bdot(p_w.astype(BF16), jnp.stack(vws, axis=0)) + _bdot(p_c.astype(BF16), jnp.stack(vcs, axis=0))) / l
    lane = lax.broadcasted_iota(jnp.int32, (nq, LANES), 1)
    for rr in range(rows_per_step):
        for pair in range(n_pairs):
            e = (rr * n_pairs + pair) * 2
            o_ref[rr * nq:(rr + 1) * nq, pair * LANES:(pair + 1) * LANES] = jnp.where(lane < HEAD_DIM, o[e], o[e + 1])


def _natten(nq4, nk, nv, bias, n_ctx):
    b, t, _ = nk.shape
    rows = (t - n_ctx) // GRID_W
    assert rows >= NA_WIN_ROWS and n_ctx % TOKEN_TILE == 0 and TOKEN_TILE % GRID_W == 0
    kv = pl.BlockSpec((None, t, NA_W), lambda bi, i: (bi, 0, 0))
    return pl.pallas_call(
        functools.partial(_natten_kernel, n_ctx=n_ctx, rows=rows),
        grid=(b, t // TOKEN_TILE),
        in_specs=[
            pl.BlockSpec((None, TOKEN_TILE, NA_HEADS * LANES), lambda bi, i: (bi, i, 0)),
            kv, kv,
            pl.BlockSpec(bias.shape, lambda bi, i: (0, 0, 0, 0)),
        ],
        out_specs=pl.BlockSpec((None, TOKEN_TILE, NA_W), lambda bi, i: (bi, i, 0)),
        out_shape=jax.ShapeDtypeStruct((b, t, NA_W), F32),
        compiler_params=_cparams("parallel", "parallel"),
        name="natten",
    )(nq4, nk, nv, bias)


def _natten_bias(rpb):
    cols = np.arange(GRID_W)
    c_start = np.clip(cols - NA_WIN_COLS // 2, 0, GRID_W - NA_WIN_COLS)
    key = np.arange(GRID_W)
    valid = (key[None, :] >= c_start[:, None]) & (key[None, :] < c_start[:, None] + NA_WIN_COLS)
    dc = key[None, :] - cols[:, None] + NA_WIN_COLS - 1
    place = (valid[:, :, None] & (dc[:, :, None] == np.arange(2 * NA_WIN_COLS - 1))).astype(np.float32)
    mask = np.where(valid, 0.0, MASK_BIAS).astype(np.float32)
    rows = jnp.stack([rpb[:, o:o + NA_WIN_ROWS, :] for o in range(NA_WIN_ROWS)], axis=1)
    dense = jnp.einsum("hopd,cxd->hocpx", rows, jnp.asarray(place), precision=HI) + mask[None, None, :, None, :]
    dense = dense.reshape(NA_HEADS, NA_WIN_ROWS, GRID_W, NA_WIN_ROWS * GRID_W)
    return jnp.concatenate([dense, jnp.full_like(dense[:, :1], MASK_BIAS)], axis=1).astype(F32)


def _out_kernel(x_ref, yf_ref, yb_ref, bonus_ref, yac_ref, yal_ref, yn_ref, g_ref, mod_ref, w_ref,
                gng_ref, gnb_ref, lng_ref, lnb_ref, ones_ref, o_ref, *, alpha, n_ctx_tiles, skip_tiles):
    is_ctx = pl.program_id(1) + skip_tiles < n_ctx_tiles
    ya = jnp.where(is_ctx, yac_ref[...], yal_ref[...])
    ones = ones_ref[...]
    ys = yf_ref[...] + yb_ref[...]
    mu = _head_sums(ys, ones) * (1.0 / HEAD_DIM)
    dv = ys - mu
    var = _head_sums(dv * dv, ones) * (1.0 / HEAD_DIM)
    yr = dv * lax.rsqrt(var + GN_EPS) * gng_ref[...] + gnb_ref[...] + bonus_ref[...]
    g = g_ref[...].astype(F32)
    parts = (
        yr * _silu(g[:, 0:RWKV_W]),
        ya * _silu(g[:, RWKV_W:RWKV_W + GQA_W]),
        yn_ref[...] * _silu(g[:, RWKV_W + GQA_W:]),
    )
    mixed = jnp.concatenate([p.astype(BF16) for p in parts], axis=1)
    y = _dot(mixed, w_ref[...])
    xn = alpha * x_ref[...] + mod_ref[2:3, :] * y
    mean = jnp.mean(xn, axis=-1, keepdims=True)
    xc = xn - mean
    var = jnp.mean(xc * xc, axis=-1, keepdims=True)
    o_ref[...] = xc * lax.rsqrt(var + LN_EPS) * lng_ref[...] + lnb_ref[...]


def _output(xc, yf, yb, bonus, ya_ctx, ya_lat, yn, gates, modsel, w_out, gn_g, gn_b, ln_g, ln_b, ones_heads, alpha,
            n_ctx_tiles, skip_tiles):
    b, t, d = xc.shape
    nt = t // TOKEN_TILE - skip_tiles
    tok = lambda width: pl.BlockSpec((None, TOKEN_TILE, width), lambda bi, i: (bi, i + skip_tiles, 0))
    full = lambda shape: pl.BlockSpec(shape, lambda bi, i: (0,) * len(shape))
    ctx_tile = lambda bi, i: (bi, jnp.minimum(i + skip_tiles, n_ctx_tiles - 1), 0)
    lat_tile = lambda bi, i: (bi, jnp.maximum(i + skip_tiles - n_ctx_tiles, 0), 0)
    return pl.pallas_call(
        functools.partial(_out_kernel, alpha=alpha, n_ctx_tiles=n_ctx_tiles, skip_tiles=skip_tiles),
        grid=(b, nt),
        in_specs=[
            tok(d), tok(RWKV_W), tok(RWKV_W), tok(RWKV_W),
            pl.BlockSpec((None, TOKEN_TILE, GQA_W), ctx_tile), pl.BlockSpec((None, TOKEN_TILE, GQA_W), lat_tile),
            tok(NA_W), tok(RWKV_W + GQA_W + NA_W),
            pl.BlockSpec((None, None, 3, d), lambda bi, i: (bi, jnp.where(i + skip_tiles < n_ctx_tiles, 0, 1), 0, 0)),
            full(w_out.shape), full((1, RWKV_W)), full((1, RWKV_W)), full((1, d)), full((1, d)), full((2 * RWKV_W, RWKV_W)),
        ],
        out_specs=pl.BlockSpec((None, TOKEN_TILE, d), lambda bi, i: (bi, i, 0)),
        out_shape=jax.ShapeDtypeStruct((b, nt * TOKEN_TILE, d), F32),
        compiler_params=_cparams("parallel", "parallel"),
        name="out",
    )(xc, yf, yb, bonus, ya_ctx, ya_lat, yn, gates, modsel, w_out, gn_g, gn_b, ln_g, ln_b, ones_heads)


def _rope_tables(n_ctx, n_lat):
    t = jnp.arange(n_lat, dtype=jnp.int32)
    row = (t // GRID_W).astype(F32)
    col = (t % GRID_W).astype(F32)
    n_freq = HEAD_DIM // 4
    inv_freq = ROPE_THETA ** (-jnp.arange(n_freq, dtype=F32) / n_freq)
    ang_r = row[:, None] * inv_freq
    ang_c = col[:, None] * inv_freq
    ang = jnp.concatenate([ang_r, ang_r, ang_c, ang_c], axis=-1)
    cos = jnp.concatenate([jnp.ones((n_ctx, HEAD_DIM), F32), jnp.cos(ang)], axis=0)
    sin = jnp.concatenate([jnp.zeros((n_ctx, HEAD_DIM), F32), jnp.sin(ang)], axis=0)
    first = (np.arange(HEAD_DIM) % (HEAD_DIM // 2)) < HEAD_DIM // 4
    sin_a = jnp.where(first, -sin, 0.0)
    sin_b = jnp.where(first, 0.0, sin)
    pair = lambda z: jnp.concatenate([z, z], axis=-1)
    return pair(cos), pair(sin_a), pair(sin_b)


def _block_ones(n_heads):
    blk = np.kron(np.eye(n_heads, dtype=np.float32), np.ones((HEAD_DIM, HEAD_DIM), np.float32))
    return jnp.asarray(blk)


def kernel(x, c, ctx, c_ctx, w_mod, b_mod, w_in, w_out, rwkv_conv, decay_w0, decay_w2, iclr_a0, iclr_a2, rwkv_k_k, rwkv_k_a, rwkv_r_k, rwkv_gn_g, rwkv_gn_b, gqa_q_norm, gqa_k_norm, na_rpb, ln_g, ln_b):
    b, n_lat, d = x.shape
    n_ctx = ctx.shape[1]
    depth = w_mod.shape[0]
    alpha = float((2 * depth) ** 0.25)
    n_ctx_tiles = n_ctx // TOKEN_TILE
    assert n_ctx % TOKEN_TILE == 0 and n_lat % TOKEN_TILE == 0 and n_lat % GRID_W == 0

    n_rows = -(-(b + 1) // SUBLANES) * SUBLANES
    cvec = jnp.zeros((n_rows, d), F32).at[:b].set(c).at[b].set(c_ctx)
    mod = _modulation(cvec, w_mod, b_mod).reshape(depth, n_rows, 3, d)
    mod_ctx = jnp.broadcast_to(mod[:, b][:, None], (depth, b, 3, d))
    modsel = jnp.stack([mod_ctx, mod[:, :b]], axis=2)

    sizes = (RWKV_CONV_W, RWKV_W, GQA_W, GQA_KV_W, GQA_KV_W, GQA_W, NA_W, NA_W, NA_W, NA_W)
    offs = np.concatenate([[0], np.cumsum(sizes)])
    seg = lambda n: w_in[:, :, offs[n]:offs[n + 1]]

    def pair_heads(z, axis):
        shp = z.shape[:axis] + (GQA_KV_HEADS, GQA_Q_HEADS // GQA_KV_HEADS, HEAD_DIM) + z.shape[axis + 1:]
        return jnp.swapaxes(z.reshape(shp), axis, axis + 1).reshape(z.shape)

    w_in_p = jnp.concatenate([seg(0), pair_heads(seg(2), 2), seg(3), seg(4), seg(6), seg(7), seg(8),
                              seg(1), pair_heads(seg(5), 2), seg(9)], axis=2).astype(BF16)
    w_out_p = jnp.concatenate([w_out[:, :RWKV_W], pair_heads(w_out[:, RWKV_W:RWKV_W + GQA_W], 1),
                               w_out[:, RWKV_W + GQA_W:]], axis=1).astype(BF16)

    w = RWKV_W
    zero = jnp.zeros((depth, LOW_RANK, w), F32)
    blocks = (decay_w2[:, 0], decay_w2[:, 1], iclr_a2[:, 0], iclr_a2[:, 1])
    wlr = jnp.concatenate([jnp.concatenate([blk if cc == rr else zero for cc in range(4)], axis=2)
                           for rr, blk in enumerate(blocks)], axis=1)
    wlr_hi = wlr.astype(BF16)
    wlr_lo = (wlr - wlr_hi.astype(F32)).astype(BF16)
    wlr3 = jnp.concatenate([wlr_hi, wlr_hi, wlr_lo], axis=1)
    b0 = jnp.concatenate([decay_w0[:, 0], decay_w0[:, 1], iclr_a0[:, 0], iclr_a0[:, 1]], axis=-1)[:, None, :]

    cos, sin_a, sin_b = _rope_tables(n_ctx, n_lat)
    ones_heads = jnp.tile(_block_ones(RWKV_HEADS), (2, 1)).astype(BF16)
    ones_pair = jnp.tile(_block_ones(2), (2, 1)).astype(BF16)
    pair = lambda z: jnp.concatenate([z, z], axis=-1)[:, None, :]
    qn, kn = pair(gqa_q_norm), pair(gqa_k_norm)

    xc = jnp.concatenate([ctx, x], axis=1)
    t = n_ctx + n_lat
    kv_chunk = next(ch for ch in KV_CHUNKS if t % ch == 0)
    for l in range(depth):
        ru, aq, ak, av, nq4, nk, nv, gates = _projection(xc, modsel[l], w_in_p[l], n_ctx_tiles)
        prep = _rwkv_prep(ru, rwkv_conv[l], wlr3[l], b0[l], rwkv_k_k[l][None], rwkv_k_a[l][None],
                          rwkv_r_k[l].reshape(1, w), ones_heads, n_ctx_tiles)
        yf, yb = _rwkv_scan(prep, n_ctx)
        last = l == depth - 1
        qt, kr, vt4 = _qk_prep(aq, ak, av, cos, sin_a, sin_b, qn[l], kn[l], ones_pair, kv_chunk, n_ctx_tiles)
        k4 = kr.reshape(b, t // kv_chunk, kv_chunk, GQA_KV_W)
        ya = _gqa(qt, k4, vt4, 0, n_lat)
        if last:
            ya_c = ya
        else:
            ya_c = _gqa(qt, kr[:, :n_ctx].reshape(b, 1, n_ctx, GQA_KV_W), vt4[:, :1, :, :n_ctx], n_lat, n_ctx)
        yn = _natten(nq4, nk, nv, _natten_bias(na_rpb[l]), n_ctx)
        xc = _output(xc, yf, yb, prep[9], ya_c, ya, yn, gates, modsel[l], w_out_p[l], rwkv_gn_g[l][None],
                     rwkv_gn_b[l][None], ln_g[l][None], ln_b[l][None], ones_heads, alpha, n_ctx_tiles,
                     n_ctx_tiles if last else 0)
    return xc
```

```python
import functools

import numpy as np
import jax
import jax.numpy as jnp
from jax import lax
from jax.experimental import pallas as pl
from jax.experimental.pallas import tpu as pltpu

F32 = jnp.float32
BF16 = jnp.bfloat16
HI = lax.Precision.HIGHEST

HEAD_DIM = 64
RWKV_HEADS = 6
GQA_Q_HEADS = 6
GQA_KV_HEADS = 2
NA_HEADS = 4
RWKV_W = RWKV_HEADS * HEAD_DIM
GQA_W = GQA_Q_HEADS * HEAD_DIM
GQA_KV_W = GQA_KV_HEADS * HEAD_DIM
NA_W = NA_HEADS * HEAD_DIM
LOW_RANK = 32
RWKV_CONV_W = 3 * RWKV_W + 4 * LOW_RANK
GRID_W = 64
NA_WIN_ROWS = 8
NA_WIN_COLS = 16
ROPE_THETA = 10000.0
LN_EPS = 1e-5
RMS_EPS = 1e-6
GN_EPS = 64e-5
KK_EPS = 1e-12
MASK_BIAS = -1e30

LANES = 128
SUBLANES = 8
TOKEN_TILE = 256
SCAN_CHUNK = 64
SCAN_CHUNKS_PER_STEP = 4
KV_CHUNKS = (768, 512, 256)
GQA_Q_TILES = (1024, 512, 256)
PV_ROWS = HEAD_DIM + 16
FAST_PASS_MIN_DENOM = 2.0 ** -64
LOG2_E = 1.4426950408889634
VMEM_LIMIT = 56 * 1024 * 1024


def _cparams(*sem):
    return pltpu.CompilerParams(dimension_semantics=sem, vmem_limit_bytes=VMEM_LIMIT)


def _dot(a, b, precision=None):
    return jnp.dot(a, b, preferred_element_type=F32, precision=precision)


def _b(x):
    return x.astype(BF16)


def _split2(x):
    hi = x.astype(BF16)
    return hi, (x - hi.astype(F32)).astype(BF16)


def _head_sums(x, ones2):
    hi, lo = _split2(x)
    return _dot(jnp.concatenate([hi, lo], axis=1), ones2)


def _sigmoid(x):
    return 1.0 / (1.0 + jnp.exp(-x))


def _silu(x):
    return x * _sigmoid(x)


def _mod_kernel(c_ref, w_ref, b_ref, o_ref):
    o_ref[...] = _dot(_silu(c_ref[...]), w_ref[...], HI) + b_ref[...]


def _modulation(cvec, w_mod, b_mod):
    depth, d, d3 = w_mod.shape
    r = cvec.shape[0]
    nj = d3 // d
    return pl.pallas_call(
        _mod_kernel,
        grid=(depth, nj),
        in_specs=[
            pl.BlockSpec((r, d), lambda l, j: (0, 0)),
            pl.BlockSpec((None, d, d), lambda l, j: (l, 0, j)),
            pl.BlockSpec((None, 1, d), lambda l, j: (l, 0, j)),
        ],
        out_specs=pl.BlockSpec((None, r, d), lambda l, j: (l, 0, j)),
        out_shape=jax.ShapeDtypeStruct((depth, r, d3), F32),
        compiler_params=_cparams("parallel", "parallel"),
        name="modulation",
    )(cvec, w_mod, b_mod.reshape(depth, 1, d3))


_PROJ_SLABS = (
    ("ru", RWKV_CONV_W, F32),
    ("aq", GQA_W, F32),
    ("ak", GQA_KV_W, F32),
    ("av", GQA_KV_W, BF16),
    ("nq", NA_W, None),
    ("nk", NA_W, BF16),
    ("nv", NA_W, BF16),
    ("gates", RWKV_W + GQA_W + NA_W, BF16),
)


def _proj_kernel(x_ref, mod_ref, w_ref, ru_ref, aq_ref, ak_ref, av_ref, nq_ref, nk_ref, nv_ref, g_ref):
    shift = mod_ref[0:1, :]
    scale = mod_ref[1:2, :]
    h = (x_ref[...] * (1.0 + scale) + shift).astype(BF16)
    outs = {"ru": ru_ref, "aq": aq_ref, "ak": ak_ref, "av": av_ref, "nk": nk_ref, "nv": nv_ref, "gates": g_ref}
    off = 0
    for name, width, _ in _PROJ_SLABS:
        u = _dot(h, w_ref[:, off:off + width])
        if name == "nq":
            u = u * (HEAD_DIM ** -0.5)
            lane = lax.broadcasted_iota(jnp.int32, (u.shape[0], LANES), 1)
            for hd in range(NA_HEADS):
                pair = u[:, (hd // 2) * LANES:(hd // 2 + 1) * LANES]
                keep = (lane < HEAD_DIM) if hd % 2 == 0 else (lane >= HEAD_DIM)
                nq_ref[:, hd * LANES:(hd + 1) * LANES] = jnp.where(keep, pair, 0.0).astype(BF16)
        else:
            outs[name][...] = u.astype(outs[name].dtype)
        off += width


def _layer_spec(l, shape):
    return pl.BlockSpec((None,) + tuple(shape), lambda bi, i: (l,) + (0,) * len(shape))


def _projection(xc, modsel, w_perm, l, n_ctx_tiles):
    b, t, d = xc.shape
    nt = t // TOKEN_TILE
    n_in = w_perm.shape[2]
    shapes, specs = [], []
    for name, width, dt in _PROJ_SLABS:
        if name == "nq":
            width, dt = NA_HEADS * LANES, BF16
        shapes.append(jax.ShapeDtypeStruct((b, t, width), dt))
        specs.append(pl.BlockSpec((None, TOKEN_TILE, width), lambda bi, i: (bi, i, 0)))
    return pl.pallas_call(
        _proj_kernel,
        grid=(b, nt),
        in_specs=[
            pl.BlockSpec((None, TOKEN_TILE, d), lambda bi, i: (bi, i, 0)),
            pl.BlockSpec((None, None, None, 3, d), lambda bi, i: (l, bi, jnp.where(i < n_ctx_tiles, 0, 1), 0, 0)),
            _layer_spec(l, (d, n_in)),
        ],
        out_specs=specs,
        out_shape=shapes,
        compiler_params=_cparams("parallel", "parallel"),
        name="proj",
    )(xc, modsel, w_perm)


def _rwkv_prep_kernel(x_ref, xp_ref, xn_ref, cw_ref, wlr_ref, b0_ref, kk_ref, ka_ref, rk_ref, ones_ref,
                      r_o, v_o, kk_o, lw0_o, kd0_o, bd0_o, lw1_o, kd1_o, bd1_o, bonus_o, *, n_ctx_tiles):
    i = pl.program_id(1)
    nt = pl.num_programs(1)
    tm = x_ref.shape[0]
    prev_ok = jnp.logical_and(i != 0, i != n_ctx_tiles)
    next_ok = jnp.logical_and(i != n_ctx_tiles - 1, i != nt - 1)
    x = x_ref[...]
    row = lax.broadcasted_iota(jnp.int32, x.shape, 0)
    halo_prev = jnp.where(prev_ok, xp_ref[SUBLANES - 1:SUBLANES, :], 0.0)
    halo_next = jnp.where(next_ok, xn_ref[0:1, :], 0.0)
    x_prev = jnp.where(row == 0, halo_prev, pltpu.roll(x, 1, 0))
    x_next = jnp.where(row == tm - 1, halo_next, pltpu.roll(x, tm - 1, 0))
    u = cw_ref[0:1, :] * x_prev + cw_ref[1:2, :] * x + cw_ref[2:3, :] * x_next

    w = RWKV_W
    r, k, v = u[:, 0:w], u[:, w:2 * w], u[:, 2 * w:3 * w]
    lr = u[:, 3 * w:3 * w + LANES]
    lane = lax.broadcasted_iota(jnp.int32, lr.shape, 1)
    z = jnp.where(lane < 2 * LOW_RANK, jnp.tanh(lr), lr)
    z_hi, z_lo = _split2(z)
    pre = _dot(jnp.concatenate([z_hi, z_lo, z_hi], axis=1), wlr_ref[...]) + b0_ref[...]

    ones = ones_ref[...]
    kk0 = k * kk_ref[...]
    ss = _head_sums(kk0 * kk0, ones)
    kk = kk0 * lax.rsqrt(jnp.maximum(ss, KK_EPS))
    r_o[...] = r
    v_o[...] = v
    kk_o[...] = kk
    kd_sum = None
    for d, (lw_o, kd_o, bd_o) in enumerate(((lw0_o, kd0_o, bd0_o), (lw1_o, kd1_o, bd1_o))):
        xd = -pre[:, d * w:(d + 1) * w]
        softplus = jnp.maximum(xd, 0.0) + jnp.log(1.0 + jnp.exp(-jnp.abs(xd)))
        lw_o[...] = -jnp.exp(-softplus - 0.5)
        iclr = _sigmoid(pre[:, (2 + d) * w:(3 + d) * w])
        kd = k * (1.0 + (iclr - 1.0) * ka_ref[...])
        kd_o[...] = kd
        bd_o[...] = kk * iclr
        kd_sum = kd if kd_sum is None else kd_sum + kd
    bonus_o[...] = _head_sums(r * kd_sum * rk_ref[...], ones) * v


def _rwkv_prep(ru, conv_w, wlr, b0, k_k, k_a, r_k, ones_heads, l, n_ctx_tiles):
    b, t, cw = ru.shape
    nt = t // TOKEN_TILE
    hb = TOKEN_TILE // SUBLANES
    w = RWKV_W
    full = lambda shape: pl.BlockSpec(shape, lambda bi, i: (0,) * len(shape))
    layer = functools.partial(_layer_spec, l)
    out_spec = pl.BlockSpec((None, TOKEN_TILE, w), lambda bi, i: (bi, i, 0))
    return pl.pallas_call(
        functools.partial(_rwkv_prep_kernel, n_ctx_tiles=n_ctx_tiles),
        grid=(b, nt),
        in_specs=[
            pl.BlockSpec((None, TOKEN_TILE, cw), lambda bi, i: (bi, i, 0)),
            pl.BlockSpec((None, SUBLANES, cw), lambda bi, i: (bi, jnp.maximum(i * hb - 1, 0), 0)),
            pl.BlockSpec((None, SUBLANES, cw), lambda bi, i: (bi, jnp.minimum((i + 1) * hb, t // SUBLANES - 1), 0)),
            layer((3, cw)), layer((3 * LANES, 4 * w)), layer((1, 4 * w)),
            layer((1, w)), layer((1, w)), layer((1, w)), full((2 * w, w)),
        ],
        out_specs=[out_spec] * 10,
        out_shape=[jax.ShapeDtypeStruct((b, t, w), F32)] * 10,
        compiler_params=_cparams("parallel", "parallel"),
        name="rwkv_prep",
    )(ru, ru, ru, conv_w, wlr, b0, k_k, k_a, r_k, ones_heads)


def _bdot(a, b):
    return lax.dot_general(a, b, (((2,), (1,)), ((0,), (0,))), preferred_element_type=F32)


def _bdot_nt(a, b):
    return lax.dot_general(a, b, (((2,), (2,)), ((0,), (0,))), preferred_element_type=F32)


def _bdot_tn(a, b):
    return lax.dot_general(a, b, (((1,), (1,)), ((0,), (0,))), preferred_element_type=F32)


def _chunk_local(r, k, v, kap, bet, lw, consts):
    tri, rev, strict, incl, head_rows, eye2, eye_l = consts
    c = r.shape[1]
    l1 = lw.astype(BF16)
    l2 = (lw - l1.astype(F32)).astype(BF16)
    l3 = (lw - l1.astype(F32) - l2.astype(F32)).astype(BF16)
    lcs = _bdot(tri, jnp.concatenate([l1, l2, l3], axis=2))
    lc = lcs[:, :, 0:LANES] + lcs[:, :, LANES:2 * LANES] + lcs[:, :, 2 * LANES:3 * LANES]
    ltot = jnp.where(rev, lc[:, 0:1, :], lc[:, c - 1:c, :])
    e_in = jnp.exp(lc)
    e_inv = jnp.exp(-lc)
    e_hat = jnp.exp(ltot - lc)

    def stack(x):
        return jnp.where(head_rows, jnp.concatenate([x, x], axis=1), 0.0)

    r_s = stack(r * e_in)
    kap_s = stack(kap * jnp.exp(lc - lw))
    k_s = stack(k * e_inv)
    b_s = stack(bet * e_inv)
    kh_s = stack(k * e_hat)
    bh_s = stack(bet * e_hat)
    v_s = stack(v)

    m = _bdot_nt(_b(jnp.concatenate([kap_s, r_s], axis=1)), _b(jnp.concatenate([b_s, k_s], axis=1)))
    c2 = 2 * c
    a = jnp.where(strict, m[:, 0:c2, 0:c2], 0.0)
    bm = jnp.where(strict, m[:, 0:c2, c2:2 * c2], 0.0)
    mb = jnp.where(incl, m[:, c2:2 * c2, 0:c2], 0.0)
    mk = jnp.where(incl, m[:, c2:2 * c2, c2:2 * c2], 0.0)

    t_inv = eye2 - a
    pw = a
    for _ in range(int(np.log2(c)) - 1):
        pw_b = _b(_bdot(_b(pw), _b(pw)))
        pw = pw_b
        t_inv = t_inv + _bdot(_b(t_inv), pw_b)

    v_b = _b(v_s)
    bv = _bdot(_b(bm), v_b)
    x_b = _b(_bdot(_b(t_inv), _b(jnp.concatenate([kap_s, bv], axis=2))))
    z = _bdot(_b(mb), x_b)
    rp = r_s - z[:, :, 0:LANES]
    y0 = _bdot(_b(mk), v_b) - z[:, :, LANES:2 * LANES]
    bhx = _bdot_tn(_b(bh_s), x_b)
    g = jnp.where(eye_l, jnp.exp(ltot), 0.0) - bhx[:, :, 0:LANES]
    hadd = _bdot_tn(_b(kh_s), v_b) - bhx[:, :, LANES:2 * LANES]
    return rp, y0, g, hadd


def _rwkv_scan_kernel(rf, vf, kkf, lwf, kdf, bdf, rb, vb, kkb, lwb, kdb, bdb, yf_ref, yb_ref, h_ref):
    i = pl.program_id(1)

    @pl.when(i == 0)
    def _():
        h_ref[...] = jnp.zeros_like(h_ref)

    c = SCAN_CHUNK
    c2 = 2 * c
    n_slots = rf.shape[0] // c
    n_pairs = RWKV_W // LANES
    n_dir = 2 * n_pairs
    n = n_slots * n_dir
    iota = lambda shape, dim: lax.broadcasted_iota(jnp.int32, shape, dim)
    is_rev = lambda shape: lax.rem(iota(shape, 0), n_dir) >= n_pairs
    rev = is_rev((n, 1, 1))
    sign = lambda shape: jnp.where(is_rev(shape), 1, -1)
    tri = jnp.where((iota((n, c, c), 2) - iota((n, c, c), 1)) * sign((n, c, c)) >= 0, 1.0, 0.0).astype(BF16)
    row2, col2 = iota((n, c2, c2), 1), iota((n, c2, c2), 2)
    same = (row2 < c) == (col2 < c)
    rt = jnp.where(row2 < c, row2, row2 - c)
    ct = jnp.where(col2 < c, col2, col2 - c)
    before = (ct - rt) * sign((n, c2, c2))
    strict = jnp.logical_and(same, before > 0)
    incl = jnp.logical_and(same, before >= 0)
    head_rows = (iota((n, c2, LANES), 1) < c) == (iota((n, c2, LANES), 2) < HEAD_DIM)
    eye2 = jnp.where(row2 == col2, 1.0, 0.0)
    eye_l = iota((n, LANES, LANES), 1) == iota((n, LANES, LANES), 2)
    consts = (tri, rev, strict, incl, head_rows, eye2, eye_l)

    def rows(s, reverse):
        s = n_slots - 1 - s if reverse else s
        return slice(s * c, (s + 1) * c)

    def chains(fref, bref):
        return jnp.stack([ref[rows(s, ref is bref), p * LANES:(p + 1) * LANES]
                          for s in range(n_slots) for ref in (fref, bref) for p in range(n_pairs)], axis=0)

    rp, y0, g, hadd = _chunk_local(chains(rf, rb), chains(kdf, kdb), chains(vf, vb), chains(kkf, kkb),
                                   chains(bdf, bdb), chains(lwf, lwb), consts)
    h = h_ref[...]
    for s in range(n_slots):
        cs = slice(s * n_dir, (s + 1) * n_dir)
        h_b = _b(h)
        ys = _bdot(_b(rp[cs]), h_b) + y0[cs]
        h = _bdot(_b(g[cs]), h_b) + hadd[cs]
        y = ys[:, 0:c, :] + ys[:, c:c2, :]
        for p in range(n_pairs):
            yf_ref[rows(s, False), p * LANES:(p + 1) * LANES] = y[p]
            yb_ref[rows(s, True), p * LANES:(p + 1) * LANES] = y[n_pairs + p]
    h_ref[...] = h


def _rwkv_scan(prep, n_ctx):
    r, v, kk, lw0, kd0, bd0, lw1, kd1, bd1, _ = prep
    b, t, w = r.shape
    blk = SCAN_CHUNK * SCAN_CHUNKS_PER_STEP
    assert n_ctx % blk == 0 and t % blk == 0
    nb, ncb = t // blk, n_ctx // blk

    def bwd_block(i):
        return jnp.where(i < ncb, ncb - 1 - i, nb - 1 + ncb - i)

    fspec = pl.BlockSpec((None, blk, w), lambda bi, i: (bi, i, 0))
    bspec = pl.BlockSpec((None, blk, w), lambda bi, i: (bi, bwd_block(i), 0))
    return pl.pallas_call(
        _rwkv_scan_kernel,
        grid=(b, nb),
        in_specs=[fspec] * 6 + [bspec] * 6,
        out_specs=[fspec, bspec],
        out_shape=[jax.ShapeDtypeStruct((b, t, w), F32)] * 2,
        scratch_shapes=[pltpu.VMEM((2 * (w // LANES), LANES, LANES), F32)],
        compiler_params=_cparams("parallel", "arbitrary"),
        name="rwkv_scan",
    )(r, v, kk, lw0, kd0, bd0, r, v, kk, lw1, kd1, bd1)


def _qk_prep_kernel(aq_ref, ak_ref, av_ref, cos_ref, sa_ref, sb_ref, qn_ref, kn_ref, ones_ref, qt_o, k_o, vt_o):
    cos, sin_a, sin_b = cos_ref[...], sa_ref[...], sb_ref[...]
    ones = ones_ref[...]
    tm = cos.shape[0]
    row = lax.broadcasted_iota(jnp.int32, (LANES, tm), 0)

    def norm_rope(x, g):
        ms = _head_sums(x * x, ones) * (1.0 / HEAD_DIM)
        xn = x * lax.rsqrt(ms + RMS_EPS) * g
        return (xn * cos + pltpu.roll(xn, LANES - HEAD_DIM // 4, 1) * sin_a
                + pltpu.roll(xn, HEAD_DIM // 4, 1) * sin_b)

    for j in range(GQA_W // LANES):
        q = norm_rope(aq_ref[:, j * LANES:(j + 1) * LANES], qn_ref[...]) * (HEAD_DIM ** -0.5 * LOG2_E)
        qt = q.T
        qt_o[(2 * j) * LANES:(2 * j + 1) * LANES, :] = jnp.where(row < HEAD_DIM, qt, 0.0).astype(BF16)
        qt_o[(2 * j + 1) * LANES:(2 * j + 2) * LANES, :] = jnp.where(row >= HEAD_DIM, qt, 0.0).astype(BF16)
    k_o[...] = norm_rope(ak_ref[...], kn_ref[...]).astype(BF16)
    vt = av_ref[...].astype(F32).T
    one = jnp.ones((PV_ROWS - HEAD_DIM, tm), F32)
    vt_o[...] = jnp.concatenate([vt[0:HEAD_DIM], one, vt[HEAD_DIM:2 * HEAD_DIM], one], axis=0).astype(BF16)


def _qk_prep(aq, ak, av, cos, sin_a, sin_b, qn, kn, ones_pair, l, kv_chunk, n_ctx_tiles):
    b, t, _ = aq.shape
    nt = t // TOKEN_TILE
    per = kv_chunk // TOKEN_TILE
    tok = lambda width: pl.BlockSpec((None, TOKEN_TILE, width), lambda bi, i: (bi, i, 0))
    tab = pl.BlockSpec((TOKEN_TILE, LANES), lambda bi, i: (i, 0))
    full = lambda shape: pl.BlockSpec(shape, lambda bi, i: (0,) * len(shape))
    return pl.pallas_call(
        _qk_prep_kernel,
        grid=(b, nt),
        in_specs=[tok(GQA_W), tok(GQA_KV_W), tok(GQA_KV_W), tab, tab, tab,
                  _layer_spec(l, (1, LANES)), _layer_spec(l, (1, LANES)), full((2 * LANES, LANES))],
        out_specs=[
            pl.BlockSpec((None, 2 * GQA_W, TOKEN_TILE),
                         lambda bi, i: (bi, 0, jnp.where(i < n_ctx_tiles, i + nt - n_ctx_tiles, i - n_ctx_tiles))),
            tok(GQA_KV_W),
            pl.BlockSpec((None, None, GQA_KV_HEADS * PV_ROWS, TOKEN_TILE), lambda bi, i: (bi, i // per, 0, i % per)),
        ],
        out_shape=[
            jax.ShapeDtypeStruct((b, 2 * GQA_W, t), BF16),
            jax.ShapeDtypeStruct((b, t, GQA_KV_W), BF16),
            jax.ShapeDtypeStruct((b, t // kv_chunk, GQA_KV_HEADS * PV_ROWS, kv_chunk), BF16),
        ],
        compiler_params=_cparams("parallel", "parallel"),
        name="qk_prep",
    )(aq, ak, av, cos, sin_a, sin_b, qn, kn, ones_pair)


def _gqa_kernel(qt_ref, k_ref, vt_ref, o_ref, acc_ref):
    n_chunks = k_ref.shape[0]
    tq = qt_ref.shape[1]
    nj = GQA_W // LANES

    def q_slab(g):
        return jnp.concatenate([qt_ref[(2 * j + g) * LANES:(2 * j + g + 1) * LANES, :] for j in range(nj)], axis=1)

    def write_out():
        for j in range(nj):
            outs = []
            for g in range(GQA_KV_HEADS):
                acc = acc_ref[g, j]
                outs.append(acc[0:HEAD_DIM] / acc[HEAD_DIM:HEAD_DIM + 1])
            o_ref[:, j * LANES:(j + 1) * LANES] = jnp.concatenate(outs, axis=0).T

    def env_body(c, carry):
        kc = k_ref[c]
        return (jnp.maximum(carry[0], jnp.max(kc, axis=0, keepdims=True)),
                jnp.minimum(carry[1], jnp.min(kc, axis=0, keepdims=True)))

    k0 = k_ref[0]
    kmax, kmin = lax.fori_loop(1, n_chunks, env_body,
                               (jnp.max(k0, axis=0, keepdims=True), jnp.min(k0, axis=0, keepdims=True)))
    env = jnp.broadcast_to(jnp.concatenate([kmax, kmin], axis=1), (2 * SUBLANES, 2 * LANES))
    refs = []
    for g in range(GQA_KV_HEADS):
        qt = q_slab(g)
        zero = jnp.zeros_like(qt)
        refs.append(_dot(env, jnp.concatenate([jnp.maximum(qt, zero), jnp.minimum(qt, zero)], axis=0))[0:1, :])
    acc_ref[...] = jnp.zeros_like(acc_ref)

    def fast_body(c, carry):
        kc = k_ref[c]
        for g in range(GQA_KV_HEADS):
            p = jnp.exp2(_dot(kc, q_slab(g)) - refs[g]).astype(BF16)
            vt = vt_ref[c, g * PV_ROWS:(g + 1) * PV_ROWS, :]
            for j in range(nj):
                acc_ref[g, j] += _dot(vt, p[:, j * tq:(j + 1) * tq])
        return carry

    lax.fori_loop(0, n_chunks, fast_body, 0)
    low = acc_ref[0, 0][HEAD_DIM:HEAD_DIM + 1]
    for g in range(GQA_KV_HEADS):
        for j in range(nj):
            low = jnp.minimum(low, acc_ref[g, j][HEAD_DIM:HEAD_DIM + 1])
    fast_ok = jnp.min(low) >= FAST_PASS_MIN_DENOM

    @pl.when(fast_ok)
    def _():
        write_out()

    @pl.when(jnp.logical_not(fast_ok))
    def _():
        acc_ref[...] = jnp.zeros_like(acc_ref)

        def body(c, ms):
            kc = k_ref[c]
            new_ms = []
            for g in range(GQA_KV_HEADS):
                s = _dot(kc, q_slab(g))
                vt = vt_ref[c, g * PV_ROWS:(g + 1) * PV_ROWS, :]
                m_new = jnp.maximum(ms[g], jnp.max(s, axis=0, keepdims=True))
                alpha = jnp.exp2(ms[g] - m_new)
                p = jnp.exp2(s - m_new).astype(BF16)
                for j in range(nj):
                    sl = slice(j * tq, (j + 1) * tq)
                    acc_ref[g, j] = alpha[:, sl] * acc_ref[g, j] + _dot(vt, p[:, sl])
                new_ms.append(m_new)
            return tuple(new_ms)

        m0 = jnp.full((1, nj * tq), -jnp.inf, F32)
        lax.fori_loop(0, n_chunks, body, (m0,) * GQA_KV_HEADS)
        write_out()


def _gqa(qt, k4, vt4, q_start, n_q):
    b = qt.shape[0]
    _, n_chunks, kv_chunk, _ = k4.shape
    tq = next(w for w in GQA_Q_TILES if n_q % w == 0 and q_start % w == 0)
    q0 = q_start // tq
    return pl.pallas_call(
        _gqa_kernel,
        grid=(b, n_q // tq),
        in_specs=[
            pl.BlockSpec((None, 2 * GQA_W, tq), lambda bi, i: (bi, 0, i + q0)),
            pl.BlockSpec((None, n_chunks, kv_chunk, GQA_KV_W), lambda bi, i: (bi, 0, 0, 0)),
            pl.BlockSpec((None, n_chunks, GQA_KV_HEADS * PV_ROWS, kv_chunk), lambda bi, i: (bi, 0, 0, 0)),
        ],
        out_specs=pl.BlockSpec((None, tq, GQA_W), lambda bi, i: (bi, i, 0)),
        out_shape=jax.ShapeDtypeStruct((b, n_q, GQA_W), F32),
        scratch_shapes=[pltpu.VMEM((GQA_KV_HEADS, GQA_W // LANES, PV_ROWS, tq), F32)],
        compiler_params=_cparams("parallel", "parallel"),
        name="gqa",
    )(qt, k4, vt4)


def _natten_kernel(q_ref, k_ref, v_ref, bias_ref, o_ref, *, n_ctx, rows):
    i = pl.program_id(1)
    nq = GRID_W
    rows_per_step = q_ref.shape[0] // nq
    win = NA_WIN_ROWS * GRID_W
    is_ctx = i * rows_per_step < n_ctx // nq
    n_pairs = NA_W // LANES
    qs, kws, vws, kcs, vcs, bs = [], [], [], [], [], []
    for rr in range(rows_per_step):
        r = i * rows_per_step + rr - n_ctx // nq
        r_start = jnp.clip(r - NA_WIN_ROWS // 2, 0, rows - NA_WIN_ROWS)
        off = jnp.where(is_ctx, NA_WIN_ROWS, r_start - r + NA_WIN_ROWS - 1)
        koff = pl.multiple_of(n_ctx + r_start * GRID_W, GRID_W)
        for pair in range(n_pairs):
            sl = slice(pair * LANES, (pair + 1) * LANES)
            kw, vw = k_ref[pl.ds(koff, win), sl], v_ref[pl.ds(koff, win), sl]
            kc, vc = k_ref[0:n_ctx, sl], v_ref[0:n_ctx, sl]
            for half in range(2):
                hd = 2 * pair + half
                qs.append(q_ref[rr * nq:(rr + 1) * nq, hd * LANES:(hd + 1) * LANES])
                bs.append(bias_ref[hd, off])
                kws.append(kw)
                vws.append(vw)
                kcs.append(kc)
                vcs.append(vc)
    q = jnp.stack(qs, axis=0)
    s_w = _bdot_nt(q, jnp.stack(kws, axis=0)) + jnp.stack(bs, axis=0)
    s_c = _bdot_nt(q, jnp.stack(kcs, axis=0))
    m = jnp.maximum(jnp.max(s_w, axis=-1, keepdims=True), jnp.max(s_c, axis=-1, keepdims=True))
    p_w = jnp.exp(s_w - m)
    p_c = jnp.exp(s_c - m)
    l = jnp.sum(p_w, axis=-1, keepdims=True) + jnp.sum(p_c, axis=-1, keepdims=True)
    o = (_bdot(p_w.astype(BF16), jnp.stack(vws, axis=0)) + _bdot(p_c.astype(BF16), jnp.stack(vcs, axis=0))) / l
    lane = lax.broadcasted_iota(jnp.int32, (nq, LANES), 1)
    for rr in range(rows_per_step):
        for pair in range(n_pairs):
            e = (rr * n_pairs + pair) * 2
            o_ref[rr * nq:(rr + 1) * nq, pair * LANES:(pair + 1) * LANES] = jnp.where(lane < HEAD_DIM, o[e], o[e + 1])


def _natten(nq4, nk, nv, bias, l, n_ctx):
    b, t, _ = nk.shape
    rows = (t - n_ctx) // GRID_W
    assert rows >= NA_WIN_ROWS and n_ctx % TOKEN_TILE == 0 and TOKEN_TILE % GRID_W == 0
    kv = pl.BlockSpec((None, t, NA_W), lambda bi, i: (bi, 0, 0))
    return pl.pallas_call(
        functools.partial(_natten_kernel, n_ctx=n_ctx, rows=rows),
        grid=(b, t // TOKEN_TILE),
        in_specs=[
            pl.BlockSpec((None, TOKEN_TILE, NA_HEADS * LANES), lambda bi, i: (bi, i, 0)),
            kv, kv,
            _layer_spec(l, bias.shape[1:]),
        ],
        out_specs=pl.BlockSpec((None, TOKEN_TILE, NA_W), lambda bi, i: (bi, i, 0)),
        out_shape=jax.ShapeDtypeStruct((b, t, NA_W), F32),
        compiler_params=_cparams("parallel", "parallel"),
        name="natten",
    )(nq4, nk, nv, bias)


def _natten_bias(rpb):
    cols = np.arange(GRID_W)
    c_start = np.clip(cols - NA_WIN_COLS // 2, 0, GRID_W - NA_WIN_COLS)
    key = np.arange(GRID_W)
    valid = (key[None, :] >= c_start[:, None]) & (key[None, :] < c_start[:, None] + NA_WIN_COLS)
    dc = key[None, :] - cols[:, None] + NA_WIN_COLS - 1
    place = (valid[:, :, None] & (dc[:, :, None] == np.arange(2 * NA_WIN_COLS - 1))).astype(np.float32)
    mask = np.where(valid, 0.0, MASK_BIAS).astype(np.float32)
    rows = jnp.stack([rpb[:, :, o:o + NA_WIN_ROWS, :] for o in range(NA_WIN_ROWS)], axis=2)
    dense = jnp.einsum("lhopd,cxd->lhocpx", rows, jnp.asarray(place), precision=HI) + mask[:, None, :]
    dense = dense.reshape(rpb.shape[0], NA_HEADS, NA_WIN_ROWS, GRID_W, NA_WIN_ROWS * GRID_W)
    return jnp.concatenate([dense, jnp.full_like(dense[:, :, :1], MASK_BIAS)], axis=2).astype(F32)


def _out_kernel(x_ref, yf_ref, yb_ref, bonus_ref, yac_ref, yal_ref, yn_ref, g_ref, mod_ref, w_ref,
                gng_ref, gnb_ref, lng_ref, lnb_ref, ones_ref, o_ref, *, alpha, n_ctx_tiles, skip_tiles):
    is_ctx = pl.program_id(1) + skip_tiles < n_ctx_tiles
    ya = jnp.where(is_ctx, yac_ref[...], yal_ref[...])
    ones = ones_ref[...]
    ys = yf_ref[...] + yb_ref[...]
    mu = _head_sums(ys, ones) * (1.0 / HEAD_DIM)
    dv = ys - mu
    var = _head_sums(dv * dv, ones) * (1.0 / HEAD_DIM)
    yr = dv * lax.rsqrt(var + GN_EPS) * gng_ref[...] + gnb_ref[...] + bonus_ref[...]
    g = g_ref[...].astype(F32)
    parts = (
        yr * _silu(g[:, 0:RWKV_W]),
        ya * _silu(g[:, RWKV_W:RWKV_W + GQA_W]),
        yn_ref[...] * _silu(g[:, RWKV_W + GQA_W:]),
    )
    mixed = jnp.concatenate([p.astype(BF16) for p in parts], axis=1)
    y = _dot(mixed, w_ref[...])
    xn = alpha * x_ref[...] + mod_ref[2:3, :] * y
    mean = jnp.mean(xn, axis=-1, keepdims=True)
    xc = xn - mean
    var = jnp.mean(xc * xc, axis=-1, keepdims=True)
    o_ref[...] = xc * lax.rsqrt(var + LN_EPS) * lng_ref[...] + lnb_ref[...]


def _output(xc, yf, yb, bonus, ya_ctx, ya_lat, yn, gates, modsel, w_out, gn_g, gn_b, ln_g, ln_b, ones_heads, l, alpha,
            n_ctx_tiles, skip_tiles):
    b, t, d = xc.shape
    nt = t // TOKEN_TILE - skip_tiles
    tok = lambda width: pl.BlockSpec((None, TOKEN_TILE, width), lambda bi, i: (bi, i + skip_tiles, 0))
    full = lambda shape: pl.BlockSpec(shape, lambda bi, i: (0,) * len(shape))
    ctx_tile = lambda bi, i: (bi, jnp.minimum(i + skip_tiles, n_ctx_tiles - 1), 0)
    lat_tile = lambda bi, i: (bi, jnp.maximum(i + skip_tiles - n_ctx_tiles, 0), 0)
    return pl.pallas_call(
        functools.partial(_out_kernel, alpha=alpha, n_ctx_tiles=n_ctx_tiles, skip_tiles=skip_tiles),
        grid=(b, nt),
        in_specs=[
            tok(d), tok(RWKV_W), tok(RWKV_W), tok(RWKV_W),
            pl.BlockSpec((None, TOKEN_TILE, GQA_W), ctx_tile), pl.BlockSpec((None, TOKEN_TILE, GQA_W), lat_tile),
            tok(NA_W), tok(RWKV_W + GQA_W + NA_W),
            pl.BlockSpec((None, None, None, 3, d),
                         lambda bi, i: (l, bi, jnp.where(i + skip_tiles < n_ctx_tiles, 0, 1), 0, 0)),
            _layer_spec(l, w_out.shape[1:]), _layer_spec(l, (1, RWKV_W)), _layer_spec(l, (1, RWKV_W)),
            _layer_spec(l, (1, d)), _layer_spec(l, (1, d)), full((2 * RWKV_W, RWKV_W)),
        ],
        out_specs=pl.BlockSpec((None, TOKEN_TILE, d), lambda bi, i: (bi, i, 0)),
        out_shape=jax.ShapeDtypeStruct((b, nt * TOKEN_TILE, d), F32),
        compiler_params=_cparams("parallel", "parallel"),
        name="out",
    )(xc, yf, yb, bonus, ya_ctx, ya_lat, yn, gates, modsel, w_out, gn_g, gn_b, ln_g, ln_b, ones_heads)


def _rope_tables(n_ctx, n_lat):
    t = jnp.arange(n_lat, dtype=jnp.int32)
    row = (t // GRID_W).astype(F32)
    col = (t % GRID_W).astype(F32)
    n_freq = HEAD_DIM // 4
    inv_freq = ROPE_THETA ** (-jnp.arange(n_freq, dtype=F32) / n_freq)
    ang_r = row[:, None] * inv_freq
    ang_c = col[:, None] * inv_freq
    ang = jnp.concatenate([ang_r, ang_r, ang_c, ang_c], axis=-1)
    cos = jnp.concatenate([jnp.ones((n_ctx, HEAD_DIM), F32), jnp.cos(ang)], axis=0)
    sin = jnp.concatenate([jnp.zeros((n_ctx, HEAD_DIM), F32), jnp.sin(ang)], axis=0)
    first = (np.arange(HEAD_DIM) % (HEAD_DIM // 2)) < HEAD_DIM // 4
    sin_a = jnp.where(first, -sin, 0.0)
    sin_b = jnp.where(first, 0.0, sin)
    pair = lambda z: jnp.concatenate([z, z], axis=-1)
    return pair(cos), pair(sin_a), pair(sin_b)


def _block_ones(n_heads):
    blk = np.kron(np.eye(n_heads, dtype=np.float32), np.ones((HEAD_DIM, HEAD_DIM), np.float32))
    return jnp.asarray(blk)


def kernel(x, c, ctx, c_ctx, w_mod, b_mod, w_in, w_out, rwkv_conv, decay_w0, decay_w2, iclr_a0, iclr_a2, rwkv_k_k, rwkv_k_a, rwkv_r_k, rwkv_gn_g, rwkv_gn_b, gqa_q_norm, gqa_k_norm, na_rpb, ln_g, ln_b):
    b, n_lat, d = x.shape
    n_ctx = ctx.shape[1]
    depth = w_mod.shape[0]
    alpha = float((2 * depth) ** 0.25)
    n_ctx_tiles = n_ctx // TOKEN_TILE
    assert n_ctx % TOKEN_TILE == 0 and n_lat % TOKEN_TILE == 0 and n_lat % GRID_W == 0

    n_rows = -(-(b + 1) // SUBLANES) * SUBLANES
    cvec = jnp.zeros((n_rows, d), F32).at[:b].set(c).at[b].set(c_ctx)
    mod = _modulation(cvec, w_mod, b_mod).reshape(depth, n_rows, 3, d)
    mod_ctx = jnp.broadcast_to(mod[:, b][:, None], (depth, b, 3, d))
    modsel = jnp.stack([mod_ctx, mod[:, :b]], axis=2)

    sizes = (RWKV_CONV_W, RWKV_W, GQA_W, GQA_KV_W, GQA_KV_W, GQA_W, NA_W, NA_W, NA_W, NA_W)
    offs = np.concatenate([[0], np.cumsum(sizes)])
    seg = lambda n: w_in[:, :, offs[n]:offs[n + 1]]

    def pair_heads(z, axis):
        shp = z.shape[:axis] + (GQA_KV_HEADS, GQA_Q_HEADS // GQA_KV_HEADS, HEAD_DIM) + z.shape[axis + 1:]
        return jnp.swapaxes(z.reshape(shp), axis, axis + 1).reshape(z.shape)

    w_in_p = jnp.concatenate([seg(0), pair_heads(seg(2), 2), seg(3), seg(4), seg(6), seg(7), seg(8),
                              seg(1), pair_heads(seg(5), 2), seg(9)], axis=2).astype(BF16)
    w_out_p = jnp.concatenate([w_out[:, :RWKV_W], pair_heads(w_out[:, RWKV_W:RWKV_W + GQA_W], 1),
                               w_out[:, RWKV_W + GQA_W:]], axis=1).astype(BF16)

    w = RWKV_W
    zero = jnp.zeros((depth, LOW_RANK, w), F32)
    blocks = (decay_w2[:, 0], decay_w2[:, 1], iclr_a2[:, 0], iclr_a2[:, 1])
    wlr = jnp.concatenate([jnp.concatenate([blk if cc == rr else zero for cc in range(4)], axis=2)
                           for rr, blk in enumerate(blocks)], axis=1)
    wlr_hi = wlr.astype(BF16)
    wlr_lo = (wlr - wlr_hi.astype(F32)).astype(BF16)
    wlr3 = jnp.concatenate([wlr_hi, wlr_hi, wlr_lo], axis=1)
    b0 = jnp.concatenate([decay_w0[:, 0], decay_w0[:, 1], iclr_a0[:, 0], iclr_a0[:, 1]], axis=-1)[:, None, :]

    cos, sin_a, sin_b = _rope_tables(n_ctx, n_lat)
    ones_heads = jnp.tile(_block_ones(RWKV_HEADS), (2, 1)).astype(BF16)
    ones_pair = jnp.tile(_block_ones(2), (2, 1)).astype(BF16)
    pair = lambda z: jnp.concatenate([z, z], axis=-1)[:, None, :]
    qn, kn = pair(gqa_q_norm), pair(gqa_k_norm)

    row3 = lambda z: z.reshape(depth, 1, -1)
    k_k, k_a, r_k = row3(rwkv_k_k), row3(rwkv_k_a), row3(rwkv_r_k)
    gn_g, gn_b, lng, lnb = row3(rwkv_gn_g), row3(rwkv_gn_b), row3(ln_g), row3(ln_b)
    na_bias = _natten_bias(na_rpb)

    xc = jnp.concatenate([ctx, x], axis=1)
    t = n_ctx + n_lat
    kv_chunk = next(ch for ch in KV_CHUNKS if t % ch == 0)
    for l in range(depth):
        ru, aq, ak, av, nq4, nk, nv, gates = _projection(xc, modsel, w_in_p, l, n_ctx_tiles)
        prep = _rwkv_prep(ru, rwkv_conv, wlr3, b0, k_k, k_a, r_k, ones_heads, l, n_ctx_tiles)
        yf, yb = _rwkv_scan(prep, n_ctx)
        last = l == depth - 1
        qt, kr, vt4 = _qk_prep(aq, ak, av, cos, sin_a, sin_b, qn, kn, ones_pair, l, kv_chunk, n_ctx_tiles)
        k4 = kr.reshape(b, t // kv_chunk, kv_chunk, GQA_KV_W)
        ya = _gqa(qt, k4, vt4, 0, n_lat)
        if last:
            ya_c = ya
        else:
            ya_c = _gqa(qt, kr[:, :n_ctx].reshape(b, 1, n_ctx, GQA_KV_W), vt4[:, :1, :, :n_ctx], n_lat, n_ctx)
        yn = _natten(nq4, nk, nv, na_bias, l, n_ctx)
        xc = _output(xc, yf, yb, prep[9], ya_c, ya, yn, gates, modsel, w_out_p, gn_g, gn_b, lng, lnb, ones_heads, l,
                     alpha, n_ctx_tiles, n_ctx_tiles if last else 0)
    return xc
```

```python
import functools

import numpy as np
import jax
import jax.numpy as jnp
from jax import lax
from jax.experimental import pallas as pl
from jax.experimental.pallas import tpu as pltpu

F32 = jnp.float32
BF16 = jnp.bfloat16
HI = lax.Precision.HIGHEST

HEAD_DIM = 64
RWKV_HEADS = 6
GQA_Q_HEADS = 6
GQA_KV_HEADS = 2
NA_HEADS = 4
RWKV_W = RWKV_HEADS * HEAD_DIM
GQA_W = GQA_Q_HEADS * HEAD_DIM
GQA_KV_W = GQA_KV_HEADS * HEAD_DIM
NA_W = NA_HEADS * HEAD_DIM
LOW_RANK = 32
RWKV_CONV_W = 3 * RWKV_W + 4 * LOW_RANK
GRID_W = 64
NA_WIN_ROWS = 8
NA_WIN_COLS = 16
ROPE_THETA = 10000.0
LN_EPS = 1e-5
RMS_EPS = 1e-6
GN_EPS = 64e-5
KK_EPS = 1e-12
MASK_BIAS = -1e30

LANES = 128
SUBLANES = 8
TOKEN_TILE = 256
SCAN_CHUNK = 64
SCAN_CHUNKS_PER_STEP = 4
KV_CHUNKS = (768, 512, 256)
GQA_Q_TILES = (1024, 512, 256)
PV_ROWS = HEAD_DIM + 16
FAST_PASS_MIN_DENOM = 2.0 ** -64
LOG2_E = 1.4426950408889634
VMEM_LIMIT = 56 * 1024 * 1024


def _cparams(*sem):
    return pltpu.CompilerParams(dimension_semantics=sem, vmem_limit_bytes=VMEM_LIMIT)


def _dot(a, b, precision=None):
    return jnp.dot(a, b, preferred_element_type=F32, precision=precision)


def _b(x):
    return x.astype(BF16)


def _split2(x):
    hi = x.astype(BF16)
    return hi, (x - hi.astype(F32)).astype(BF16)


def _head_sums(x, ones2):
    hi, lo = _split2(x)
    return _dot(jnp.concatenate([hi, lo], axis=1), ones2)


def _sigmoid(x):
    return 1.0 / (1.0 + jnp.exp(-x))


def _silu(x):
    return x * _sigmoid(x)


def _mod_kernel(c_ref, w_ref, b_ref, o_ref):
    o_ref[...] = _dot(_silu(c_ref[...]), w_ref[...], HI) + b_ref[...]


def _modulation(cvec, w_mod, b_mod):
    depth, d, d3 = w_mod.shape
    r = cvec.shape[0]
    nj = d3 // d
    return pl.pallas_call(
        _mod_kernel,
        grid=(depth, nj),
        in_specs=[
            pl.BlockSpec((r, d), lambda l, j: (0, 0)),
            pl.BlockSpec((None, d, d), lambda l, j: (l, 0, j)),
            pl.BlockSpec((None, 1, d), lambda l, j: (l, 0, j)),
        ],
        out_specs=pl.BlockSpec((None, r, d), lambda l, j: (l, 0, j)),
        out_shape=jax.ShapeDtypeStruct((depth, r, d3), F32),
        compiler_params=_cparams("parallel", "parallel"),
        name="modulation",
    )(cvec, w_mod, b_mod.reshape(depth, 1, d3))


_PROJ_SLABS = (
    ("ru", RWKV_CONV_W, F32),
    ("aq", GQA_W, F32),
    ("ak", GQA_KV_W, F32),
    ("av", GQA_KV_W, BF16),
    ("nq", NA_W, None),
    ("nk", NA_W, BF16),
    ("nv", NA_W, BF16),
    ("gates", RWKV_W + GQA_W + NA_W, BF16),
)


def _proj_kernel(x_ref, mod_ref, w_ref, ru_ref, aq_ref, ak_ref, av_ref, nq_ref, nk_ref, nv_ref, g_ref):
    shift = mod_ref[0:1, :]
    scale = mod_ref[1:2, :]
    h = (x_ref[...] * (1.0 + scale) + shift).astype(BF16)
    outs = {"ru": ru_ref, "aq": aq_ref, "ak": ak_ref, "av": av_ref, "nk": nk_ref, "nv": nv_ref, "gates": g_ref}
    off = 0
    for name, width, _ in _PROJ_SLABS:
        u = _dot(h, w_ref[:, off:off + width])
        if name == "nq":
            u = u * (HEAD_DIM ** -0.5)
            lane = lax.broadcasted_iota(jnp.int32, (u.shape[0], LANES), 1)
            for hd in range(NA_HEADS):
                pair = u[:, (hd // 2) * LANES:(hd // 2 + 1) * LANES]
                keep = (lane < HEAD_DIM) if hd % 2 == 0 else (lane >= HEAD_DIM)
                nq_ref[:, hd * LANES:(hd + 1) * LANES] = jnp.where(keep, pair, 0.0).astype(BF16)
        else:
            outs[name][...] = u.astype(outs[name].dtype)
        off += width


def _projection(xc, modsel, w_perm, n_ctx_tiles):
    b, t, d = xc.shape
    nt = t // TOKEN_TILE
    n_in = w_perm.shape[1]
    shapes, specs = [], []
    for name, width, dt in _PROJ_SLABS:
        if name == "nq":
            width, dt = NA_HEADS * LANES, BF16
        shapes.append(jax.ShapeDtypeStruct((b, t, width), dt))
        specs.append(pl.BlockSpec((None, TOKEN_TILE, width), lambda bi, i: (bi, i, 0)))
    return pl.pallas_call(
        _proj_kernel,
        grid=(b, nt),
        in_specs=[
            pl.BlockSpec((None, TOKEN_TILE, d), lambda bi, i: (bi, i, 0)),
            pl.BlockSpec((None, None, 3, d), lambda bi, i: (bi, jnp.where(i < n_ctx_tiles, 0, 1), 0, 0)),
            pl.BlockSpec((d, n_in), lambda bi, i: (0, 0)),
        ],
        out_specs=specs,
        out_shape=shapes,
        compiler_params=_cparams("parallel", "parallel"),
        name="proj",
    )(xc, modsel, w_perm)


def _rwkv_prep_kernel(x_ref, xp_ref, xn_ref, cw_ref, wlr_ref, b0_ref, kk_ref, ka_ref, rk_ref, ones_ref,
                      r_o, v_o, kk_o, lw0_o, kd0_o, bd0_o, lw1_o, kd1_o, bd1_o, bonus_o, *, n_ctx_tiles):
    i = pl.program_id(1)
    nt = pl.num_programs(1)
    tm = x_ref.shape[0]
    prev_ok = jnp.logical_and(i != 0, i != n_ctx_tiles)
    next_ok = jnp.logical_and(i != n_ctx_tiles - 1, i != nt - 1)
    x = x_ref[...]
    row = lax.broadcasted_iota(jnp.int32, x.shape, 0)
    halo_prev = jnp.where(prev_ok, xp_ref[SUBLANES - 1:SUBLANES, :], 0.0)
    halo_next = jnp.where(next_ok, xn_ref[0:1, :], 0.0)
    x_prev = jnp.where(row == 0, halo_prev, pltpu.roll(x, 1, 0))
    x_next = jnp.where(row == tm - 1, halo_next, pltpu.roll(x, tm - 1, 0))
    u = cw_ref[0:1, :] * x_prev + cw_ref[1:2, :] * x + cw_ref[2:3, :] * x_next

    w = RWKV_W
    r, k, v = u[:, 0:w], u[:, w:2 * w], u[:, 2 * w:3 * w]
    lr = u[:, 3 * w:3 * w + LANES]
    lane = lax.broadcasted_iota(jnp.int32, lr.shape, 1)
    z = jnp.where(lane < 2 * LOW_RANK, jnp.tanh(lr), lr)
    z_hi, z_lo = _split2(z)
    pre = _dot(jnp.concatenate([z_hi, z_lo, z_hi], axis=1), wlr_ref[...]) + b0_ref[...]

    ones = ones_ref[...]
    kk0 = k * kk_ref[...]
    ss = _head_sums(kk0 * kk0, ones)
    kk = kk0 * lax.rsqrt(jnp.maximum(ss, KK_EPS))
    r_o[...] = r
    v_o[...] = v
    kk_o[...] = kk
    kd_sum = None
    for d, (lw_o, kd_o, bd_o) in enumerate(((lw0_o, kd0_o, bd0_o), (lw1_o, kd1_o, bd1_o))):
        xd = -pre[:, d * w:(d + 1) * w]
        softplus = jnp.maximum(xd, 0.0) + jnp.log(1.0 + jnp.exp(-jnp.abs(xd)))
        lw_o[...] = -jnp.exp(-softplus - 0.5)
        iclr = _sigmoid(pre[:, (2 + d) * w:(3 + d) * w])
        kd = k * (1.0 + (iclr - 1.0) * ka_ref[...])
        kd_o[...] = kd
        bd_o[...] = kk * iclr
        kd_sum = kd if kd_sum is None else kd_sum + kd
    bonus_o[...] = _head_sums(r * kd_sum * rk_ref[...], ones) * v


def _rwkv_prep(ru, conv_w, wlr, b0, k_k, k_a, r_k, ones_heads, n_ctx_tiles):
    b, t, cw = ru.shape
    nt = t // TOKEN_TILE
    hb = TOKEN_TILE // SUBLANES
    w = RWKV_W
    full = lambda shape: pl.BlockSpec(shape, lambda bi, i: (0,) * len(shape))
    out_spec = pl.BlockSpec((None, TOKEN_TILE, w), lambda bi, i: (bi, i, 0))
    return pl.pallas_call(
        functools.partial(_rwkv_prep_kernel, n_ctx_tiles=n_ctx_tiles),
        grid=(b, nt),
        in_specs=[
            pl.BlockSpec((None, TOKEN_TILE, cw), lambda bi, i: (bi, i, 0)),
            pl.BlockSpec((None, SUBLANES, cw), lambda bi, i: (bi, jnp.maximum(i * hb - 1, 0), 0)),
            pl.BlockSpec((None, SUBLANES, cw), lambda bi, i: (bi, jnp.minimum((i + 1) * hb, t // SUBLANES - 1), 0)),
            full((3, cw)), full((3 * LANES, 4 * w)), full((1, 4 * w)),
            full((1, w)), full((1, w)), full((1, w)), full((2 * w, w)),
        ],
        out_specs=[out_spec] * 10,
        out_shape=[jax.ShapeDtypeStruct((b, t, w), F32)] * 10,
        compiler_params=_cparams("parallel", "parallel"),
        name="rwkv_prep",
    )(ru, ru, ru, conv_w, wlr, b0, k_k, k_a, r_k, ones_heads)


def _bdot(a, b):
    return lax.dot_general(a, b, (((2,), (1,)), ((0,), (0,))), preferred_element_type=F32)


def _bdot_nt(a, b):
    return lax.dot_general(a, b, (((2,), (2,)), ((0,), (0,))), preferred_element_type=F32)


def _bdot_tn(a, b):
    return lax.dot_general(a, b, (((1,), (1,)), ((0,), (0,))), preferred_element_type=F32)


def _chunk_local(r, k, v, kap, bet, lw, consts):
    tri, rev, strict, incl, head_rows, eye2, eye_l = consts
    c = r.shape[1]
    l1 = lw.astype(BF16)
    l2 = (lw - l1.astype(F32)).astype(BF16)
    l3 = (lw - l1.astype(F32) - l2.astype(F32)).astype(BF16)
    lcs = _bdot(tri, jnp.concatenate([l1, l2, l3], axis=2))
    lc = lcs[:, :, 0:LANES] + lcs[:, :, LANES:2 * LANES] + lcs[:, :, 2 * LANES:3 * LANES]
    ltot = jnp.where(rev, lc[:, 0:1, :], lc[:, c - 1:c, :])
    e_in = jnp.exp(lc)
    e_inv = jnp.exp(-lc)
    e_hat = jnp.exp(ltot - lc)

    def stack(x):
        return jnp.where(head_rows, jnp.concatenate([x, x], axis=1), 0.0)

    r_s = stack(r * e_in)
    kap_s = stack(kap * jnp.exp(lc - lw))
    k_s = stack(k * e_inv)
    b_s = stack(bet * e_inv)
    kh_s = stack(k * e_hat)
    bh_s = stack(bet * e_hat)
    v_s = stack(v)

    m = _bdot_nt(_b(jnp.concatenate([kap_s, r_s], axis=1)), _b(jnp.concatenate([b_s, k_s], axis=1)))
    c2 = 2 * c
    a = jnp.where(strict, m[:, 0:c2, 0:c2], 0.0)
    bm = jnp.where(strict, m[:, 0:c2, c2:2 * c2], 0.0)
    mb = jnp.where(incl, m[:, c2:2 * c2, 0:c2], 0.0)
    mk = jnp.where(incl, m[:, c2:2 * c2, c2:2 * c2], 0.0)

    t_inv = eye2 - a
    pw = a
    for _ in range(int(np.log2(c)) - 1):
        pw_b = _b(_bdot(_b(pw), _b(pw)))
        pw = pw_b
        t_inv = t_inv + _bdot(_b(t_inv), pw_b)

    v_b = _b(v_s)
    bv = _bdot(_b(bm), v_b)
    x_b = _b(_bdot(_b(t_inv), _b(jnp.concatenate([kap_s, bv], axis=2))))
    z = _bdot(_b(mb), x_b)
    rp = r_s - z[:, :, 0:LANES]
    y0 = _bdot(_b(mk), v_b) - z[:, :, LANES:2 * LANES]
    bhx = _bdot_tn(_b(bh_s), x_b)
    g = jnp.where(eye_l, jnp.exp(ltot), 0.0) - bhx[:, :, 0:LANES]
    hadd = _bdot_tn(_b(kh_s), v_b) - bhx[:, :, LANES:2 * LANES]
    return rp, y0, g, hadd


def _rwkv_scan_kernel(rf, vf, kkf, lwf, kdf, bdf, rb, vb, kkb, lwb, kdb, bdb, yf_ref, yb_ref, h_ref):
    i = pl.program_id(1)

    @pl.when(i == 0)
    def _():
        h_ref[...] = jnp.zeros_like(h_ref)

    c = SCAN_CHUNK
    c2 = 2 * c
    n_slots = rf.shape[0] // c
    n_pairs = RWKV_W // LANES
    n_dir = 2 * n_pairs
    n = n_slots * n_dir
    iota = lambda shape, dim: lax.broadcasted_iota(jnp.int32, shape, dim)
    is_rev = lambda shape: lax.rem(iota(shape, 0), n_dir) >= n_pairs
    rev = is_rev((n, 1, 1))
    sign = lambda shape: jnp.where(is_rev(shape), 1, -1)
    tri = jnp.where((iota((n, c, c), 2) - iota((n, c, c), 1)) * sign((n, c, c)) >= 0, 1.0, 0.0).astype(BF16)
    row2, col2 = iota((n, c2, c2), 1), iota((n, c2, c2), 2)
    same = (row2 < c) == (col2 < c)
    rt = jnp.where(row2 < c, row2, row2 - c)
    ct = jnp.where(col2 < c, col2, col2 - c)
    before = (ct - rt) * sign((n, c2, c2))
    strict = jnp.logical_and(same, before > 0)
    incl = jnp.logical_and(same, before >= 0)
    head_rows = (iota((n, c2, LANES), 1) < c) == (iota((n, c2, LANES), 2) < HEAD_DIM)
    eye2 = jnp.where(row2 == col2, 1.0, 0.0)
    eye_l = iota((n, LANES, LANES), 1) == iota((n, LANES, LANES), 2)
    consts = (tri, rev, strict, incl, head_rows, eye2, eye_l)

    def rows(s, reverse):
        s = n_slots - 1 - s if reverse else s
        return slice(s * c, (s + 1) * c)

    def chains(fref, bref):
        return jnp.stack([ref[rows(s, ref is bref), p * LANES:(p + 1) * LANES]
                          for s in range(n_slots) for ref in (fref, bref) for p in range(n_pairs)], axis=0)

    rp, y0, g, hadd = _chunk_local(chains(rf, rb), chains(kdf, kdb), chains(vf, vb), chains(kkf, kkb),
                                   chains(bdf, bdb), chains(lwf, lwb), consts)
    h = h_ref[...]
    for s in range(n_slots):
        cs = slice(s * n_dir, (s + 1) * n_dir)
        h_b = _b(h)
        ys = _bdot(_b(rp[cs]), h_b) + y0[cs]
        h = _bdot(_b(g[cs]), h_b) + hadd[cs]
        y = ys[:, 0:c, :] + ys[:, c:c2, :]
        for p in range(n_pairs):
            yf_ref[rows(s, False), p * LANES:(p + 1) * LANES] = y[p]
            yb_ref[rows(s, True), p * LANES:(p + 1) * LANES] = y[n_pairs + p]
    h_ref[...] = h


def _rwkv_scan(prep, n_ctx):
    r, v, kk, lw0, kd0, bd0, lw1, kd1, bd1, _ = prep
    b, t, w = r.shape
    blk = SCAN_CHUNK * SCAN_CHUNKS_PER_STEP
    assert n_ctx % blk == 0 and t % blk == 0
    nb, ncb = t // blk, n_ctx // blk

    def bwd_block(i):
        return jnp.where(i < ncb, ncb - 1 - i, nb - 1 + ncb - i)

    fspec = pl.BlockSpec((None, blk, w), lambda bi, i: (bi, i, 0))
    bspec = pl.BlockSpec((None, blk, w), lambda bi, i: (bi, bwd_block(i), 0))
    return pl.pallas_call(
        _rwkv_scan_kernel,
        grid=(b, nb),
        in_specs=[fspec] * 6 + [bspec] * 6,
        out_specs=[fspec, bspec],
        out_shape=[jax.ShapeDtypeStruct((b, t, w), F32)] * 2,
        scratch_shapes=[pltpu.VMEM((2 * (w // LANES), LANES, LANES), F32)],
        compiler_params=_cparams("parallel", "arbitrary"),
        name="rwkv_scan",
    )(r, v, kk, lw0, kd0, bd0, r, v, kk, lw1, kd1, bd1)


def _qk_prep_kernel(aq_ref, ak_ref, av_ref, cos_ref, sa_ref, sb_ref, qn_ref, kn_ref, ones_ref, qt_o, k_o, vt_o):
    cos, sin_a, sin_b = cos_ref[...], sa_ref[...], sb_ref[...]
    ones = ones_ref[...]
    tm = cos.shape[0]
    row = lax.broadcasted_iota(jnp.int32, (LANES, tm), 0)

    def norm_rope(x, g):
        ms = _head_sums(x * x, ones) * (1.0 / HEAD_DIM)
        xn = x * lax.rsqrt(ms + RMS_EPS) * g
        return (xn * cos + pltpu.roll(xn, LANES - HEAD_DIM // 4, 1) * sin_a
                + pltpu.roll(xn, HEAD_DIM // 4, 1) * sin_b)

    for j in range(GQA_W // LANES):
        q = norm_rope(aq_ref[:, j * LANES:(j + 1) * LANES], qn_ref[...]) * (HEAD_DIM ** -0.5 * LOG2_E)
        qt = q.T
        qt_o[(2 * j) * LANES:(2 * j + 1) * LANES, :] = jnp.where(row < HEAD_DIM, qt, 0.0).astype(BF16)
        qt_o[(2 * j + 1) * LANES:(2 * j + 2) * LANES, :] = jnp.where(row >= HEAD_DIM, qt, 0.0).astype(BF16)
    k_o[...] = norm_rope(ak_ref[...], kn_ref[...]).astype(BF16)
    vt = av_ref[...].astype(F32).T
    one = jnp.ones((PV_ROWS - HEAD_DIM, tm), F32)
    vt_o[...] = jnp.concatenate([vt[0:HEAD_DIM], one, vt[HEAD_DIM:2 * HEAD_DIM], one], axis=0).astype(BF16)


def _qk_prep(aq, ak, av, cos, sin_a, sin_b, qn, kn, ones_pair, kv_chunk, n_ctx_tiles):
    b, t, _ = aq.shape
    nt = t // TOKEN_TILE
    per = kv_chunk // TOKEN_TILE
    tok = lambda width: pl.BlockSpec((None, TOKEN_TILE, width), lambda bi, i: (bi, i, 0))
    tab = pl.BlockSpec((TOKEN_TILE, LANES), lambda bi, i: (i, 0))
    full = lambda shape: pl.BlockSpec(shape, lambda bi, i: (0,) * len(shape))
    return pl.pallas_call(
        _qk_prep_kernel,
        grid=(b, nt),
        in_specs=[tok(GQA_W), tok(GQA_KV_W), tok(GQA_KV_W), tab, tab, tab,
                  full((1, LANES)), full((1, LANES)), full((2 * LANES, LANES))],
        out_specs=[
            pl.BlockSpec((None, 2 * GQA_W, TOKEN_TILE),
                         lambda bi, i: (bi, 0, jnp.where(i < n_ctx_tiles, i + nt - n_ctx_tiles, i - n_ctx_tiles))),
            tok(GQA_KV_W),
            pl.BlockSpec((None, None, GQA_KV_HEADS * PV_ROWS, TOKEN_TILE), lambda bi, i: (bi, i // per, 0, i % per)),
        ],
        out_shape=[
            jax.ShapeDtypeStruct((b, 2 * GQA_W, t), BF16),
            jax.ShapeDtypeStruct((b, t, GQA_KV_W), BF16),
            jax.ShapeDtypeStruct((b, t // kv_chunk, GQA_KV_HEADS * PV_ROWS, kv_chunk), BF16),
        ],
        compiler_params=_cparams("parallel", "parallel"),
        name="qk_prep",
    )(aq, ak, av, cos, sin_a, sin_b, qn, kn, ones_pair)


def _gqa_kernel(qt_ref, k_ref, vt_ref, o_ref, acc_ref):
    n_chunks = k_ref.shape[0]
    tq = qt_ref.shape[1]
    nj = GQA_W // LANES

    def q_slab(g):
        return jnp.concatenate([qt_ref[(2 * j + g) * LANES:(2 * j + g + 1) * LANES, :] for j in range(nj)], axis=1)

    def write_out():
        for j in range(nj):
            outs = []
            for g in range(GQA_KV_HEADS):
                acc = acc_ref[g, j]
                outs.append(acc[0:HEAD_DIM] / acc[HEAD_DIM:HEAD_DIM + 1])
            o_ref[:, j * LANES:(j + 1) * LANES] = jnp.concatenate(outs, axis=0).T

    def env_body(c, carry):
        kc = k_ref[c]
        return (jnp.maximum(carry[0], jnp.max(kc, axis=0, keepdims=True)),
                jnp.minimum(carry[1], jnp.min(kc, axis=0, keepdims=True)))

    k0 = k_ref[0]
    kmax, kmin = lax.fori_loop(1, n_chunks, env_body,
                               (jnp.max(k0, axis=0, keepdims=True), jnp.min(k0, axis=0, keepdims=True)))
    env = jnp.broadcast_to(jnp.concatenate([kmax, kmin], axis=1), (2 * SUBLANES, 2 * LANES))
    refs = []
    for g in range(GQA_KV_HEADS):
        qt = q_slab(g)
        zero = jnp.zeros_like(qt)
        refs.append(_dot(env, jnp.concatenate([jnp.maximum(qt, zero), jnp.minimum(qt, zero)], axis=0))[0:1, :])
    acc_ref[...] = jnp.zeros_like(acc_ref)

    def fast_body(c, carry):
        kc = k_ref[c]
        for g in range(GQA_KV_HEADS):
            vt = vt_ref[c, g * PV_ROWS:(g + 1) * PV_ROWS, :]
            for j in range(nj):
                qt = qt_ref[(2 * j + g) * LANES:(2 * j + g + 1) * LANES, :]
                p = jnp.exp2(_dot(kc, qt) - refs[g][:, j * tq:(j + 1) * tq]).astype(BF16)
                acc_ref[g, j] += _dot(vt, p)
        return carry

    lax.fori_loop(0, n_chunks, fast_body, 0)
    low = acc_ref[0, 0][HEAD_DIM:HEAD_DIM + 1]
    for g in range(GQA_KV_HEADS):
        for j in range(nj):
            low = jnp.minimum(low, acc_ref[g, j][HEAD_DIM:HEAD_DIM + 1])
    fast_ok = jnp.min(low) >= FAST_PASS_MIN_DENOM

    @pl.when(fast_ok)
    def _():
        write_out()

    @pl.when(jnp.logical_not(fast_ok))
    def _():
        acc_ref[...] = jnp.zeros_like(acc_ref)

        def body(c, ms):
            kc = k_ref[c]
            new_ms = []
            for g in range(GQA_KV_HEADS):
                s = _dot(kc, q_slab(g))
                vt = vt_ref[c, g * PV_ROWS:(g + 1) * PV_ROWS, :]
                m_new = jnp.maximum(ms[g], jnp.max(s, axis=0, keepdims=True))
                alpha = jnp.exp2(ms[g] - m_new)
                p = jnp.exp2(s - m_new).astype(BF16)
                for j in range(nj):
                    sl = slice(j * tq, (j + 1) * tq)
                    acc_ref[g, j] = alpha[:, sl] * acc_ref[g, j] + _dot(vt, p[:, sl])
                new_ms.append(m_new)
            return tuple(new_ms)

        m0 = jnp.full((1, nj * tq), -jnp.inf, F32)
        lax.fori_loop(0, n_chunks, body, (m0,) * GQA_KV_HEADS)
        write_out()


def _gqa(qt, k4, vt4, q_start, n_q):
    b = qt.shape[0]
    _, n_chunks, kv_chunk, _ = k4.shape
    tq = next(w for w in GQA_Q_TILES if n_q % w == 0 and q_start % w == 0)
    q0 = q_start // tq
    return pl.pallas_call(
        _gqa_kernel,
        grid=(b, n_q // tq),
        in_specs=[
            pl.BlockSpec((None, 2 * GQA_W, tq), lambda bi, i: (bi, 0, i + q0)),
            pl.BlockSpec((None, n_chunks, kv_chunk, GQA_KV_W), lambda bi, i: (bi, 0, 0, 0)),
            pl.BlockSpec((None, n_chunks, GQA_KV_HEADS * PV_ROWS, kv_chunk), lambda bi, i: (bi, 0, 0, 0)),
        ],
        out_specs=pl.BlockSpec((None, tq, GQA_W), lambda bi, i: (bi, i, 0)),
        out_shape=jax.ShapeDtypeStruct((b, n_q, GQA_W), F32),
        scratch_shapes=[pltpu.VMEM((GQA_KV_HEADS, GQA_W // LANES, PV_ROWS, tq), F32)],
        compiler_params=_cparams("parallel", "parallel"),
        name="gqa",
    )(qt, k4, vt4)


def _natten_kernel(q_ref, k_ref, v_ref, bias_ref, o_ref, *, n_ctx, rows):
    i = pl.program_id(1)
    nq = GRID_W
    rows_per_step = q_ref.shape[0] // nq
    win = NA_WIN_ROWS * GRID_W
    is_ctx = i * rows_per_step < n_ctx // nq
    n_pairs = NA_W // LANES
    qs, kws, vws, kcs, vcs, bs = [], [], [], [], [], []
    for rr in range(rows_per_step):
        r = i * rows_per_step + rr - n_ctx // nq
        r_start = jnp.clip(r - NA_WIN_ROWS // 2, 0, rows - NA_WIN_ROWS)
        off = jnp.where(is_ctx, NA_WIN_ROWS, r_start - r + NA_WIN_ROWS - 1)
        koff = pl.multiple_of(n_ctx + r_start * GRID_W, GRID_W)
        for pair in range(n_pairs):
            sl = slice(pair * LANES, (pair + 1) * LANES)
            kw, vw = k_ref[pl.ds(koff, win), sl], v_ref[pl.ds(koff, win), sl]
            kc, vc = k_ref[0:n_ctx, sl], v_ref[0:n_ctx, sl]
            for half in range(2):
                hd = 2 * pair + half
                qs.append(q_ref[rr * nq:(rr + 1) * nq, hd * LANES:(hd + 1) * LANES])
                bs.append(bias_ref[hd, off])
                kws.append(kw)
                vws.append(vw)
                kcs.append(kc)
                vcs.append(vc)
    q = jnp.stack(qs, axis=0)
    s_w = _bdot_nt(q, jnp.stack(kws, axis=0)) + jnp.stack(bs, axis=0)
    s_c = _bdot_nt(q, jnp.stack(kcs, axis=0))
    m = jnp.maximum(jnp.max(s_w, axis=-1, keepdims=True), jnp.max(s_c, axis=-1, keepdims=True))
    p_w = jnp.exp(s_w - m)
    p_c = jnp.exp(s_c - m)
    l = jnp.sum(p_w, axis=-1, keepdims=True) + jnp.sum(p_c, axis=-1, keepdims=True)
    o = (_bdot(p_w.astype(BF16), jnp.stack(vws, axis=0)) + _bdot(p_c.astype(BF16), jnp.stack(vcs, axis=0))) / l
    lane = lax.broadcasted_iota(jnp.int32, (nq, LANES), 1)
    for rr in range(rows_per_step):
        for pair in range(n_pairs):
            e = (rr * n_pairs + pair) * 2
            o_ref[rr * nq:(rr + 1) * nq, pair * LANES:(pair + 1) * LANES] = jnp.where(lane < HEAD_DIM, o[e], o[e + 1])


def _natten(nq4, nk, nv, bias, n_ctx):
    b, t, _ = nk.shape
    rows = (t - n_ctx) // GRID_W
    assert rows >= NA_WIN_ROWS and n_ctx % TOKEN_TILE == 0 and TOKEN_TILE % GRID_W == 0
    kv = pl.BlockSpec((None, t, NA_W), lambda bi, i: (bi, 0, 0))
    return pl.pallas_call(
        functools.partial(_natten_kernel, n_ctx=n_ctx, rows=rows),
        grid=(b, t // TOKEN_TILE),
        in_specs=[
            pl.BlockSpec((None, TOKEN_TILE, NA_HEADS * LANES), lambda bi, i: (bi, i, 0)),
            kv, kv,
            pl.BlockSpec(bias.shape, lambda bi, i: (0, 0, 0, 0)),
        ],
        out_specs=pl.BlockSpec((None, TOKEN_TILE, NA_W), lambda bi, i: (bi, i, 0)),
        out_shape=jax.ShapeDtypeStruct((b, t, NA_W), F32),
        compiler_params=_cparams("parallel", "parallel"),
        name="natten",
    )(nq4, nk, nv, bias)


def _natten_bias(rpb):
    cols = np.arange(GRID_W)
    c_start = np.clip(cols - NA_WIN_COLS // 2, 0, GRID_W - NA_WIN_COLS)
    key = np.arange(GRID_W)
    valid = (key[None, :] >= c_start[:, None]) & (key[None, :] < c_start[:, None] + NA_WIN_COLS)
    dc = key[None, :] - cols[:, None] + NA_WIN_COLS - 1
    place = (valid[:, :, None] & (dc[:, :, None] == np.arange(2 * NA_WIN_COLS - 1))).astype(np.float32)
    mask = np.where(valid, 0.0, MASK_BIAS).astype(np.float32)
    rows = jnp.stack([rpb[:, o:o + NA_WIN_ROWS, :] for o in range(NA_WIN_ROWS)], axis=1)
    dense = jnp.einsum("hopd,cxd->hocpx", rows, jnp.asarray(place), precision=HI) + mask[None, None, :, None, :]
    dense = dense.reshape(NA_HEADS, NA_WIN_ROWS, GRID_W, NA_WIN_ROWS * GRID_W)
    return jnp.concatenate([dense, jnp.full_like(dense[:, :1], MASK_BIAS)], axis=1).astype(F32)


def _out_kernel(x_ref, yf_ref, yb_ref, bonus_ref, yac_ref, yal_ref, yn_ref, g_ref, mod_ref, w_ref,
                gng_ref, gnb_ref, lng_ref, lnb_ref, ones_ref, o_ref, *, alpha, n_ctx_tiles, skip_tiles):
    is_ctx = pl.program_id(1) + skip_tiles < n_ctx_tiles
    ya = jnp.where(is_ctx, yac_ref[...], yal_ref[...])
    ones = ones_ref[...]
    ys = yf_ref[...] + yb_ref[...]
    mu = _head_sums(ys, ones) * (1.0 / HEAD_DIM)
    dv = ys - mu
    var = _head_sums(dv * dv, ones) * (1.0 / HEAD_DIM)
    yr = dv * lax.rsqrt(var + GN_EPS) * gng_ref[...] + gnb_ref[...] + bonus_ref[...]
    g = g_ref[...].astype(F32)
    parts = (
        yr * _silu(g[:, 0:RWKV_W]),
        ya * _silu(g[:, RWKV_W:RWKV_W + GQA_W]),
        yn_ref[...] * _silu(g[:, RWKV_W + GQA_W:]),
    )
    mixed = jnp.concatenate([p.astype(BF16) for p in parts], axis=1)
    y = _dot(mixed, w_ref[...])
    xn = alpha * x_ref[...] + mod_ref[2:3, :] * y
    mean = jnp.mean(xn, axis=-1, keepdims=True)
    xc = xn - mean
    var = jnp.mean(xc * xc, axis=-1, keepdims=True)
    o_ref[...] = xc * lax.rsqrt(var + LN_EPS) * lng_ref[...] + lnb_ref[...]


def _output(xc, yf, yb, bonus, ya_ctx, ya_lat, yn, gates, modsel, w_out, gn_g, gn_b, ln_g, ln_b, ones_heads, alpha,
            n_ctx_tiles, skip_tiles):
    b, t, d = xc.shape
    nt = t // TOKEN_TILE - skip_tiles
    tok = lambda width: pl.BlockSpec((None, TOKEN_TILE, width), lambda bi, i: (bi, i + skip_tiles, 0))
    full = lambda shape: pl.BlockSpec(shape, lambda bi, i: (0,) * len(shape))
    ctx_tile = lambda bi, i: (bi, jnp.minimum(i + skip_tiles, n_ctx_tiles - 1), 0)
    lat_tile = lambda bi, i: (bi, jnp.maximum(i + skip_tiles - n_ctx_tiles, 0), 0)
    return pl.pallas_call(
        functools.partial(_out_kernel, alpha=alpha, n_ctx_tiles=n_ctx_tiles, skip_tiles=skip_tiles),
        grid=(b, nt),
        in_specs=[
            tok(d), tok(RWKV_W), tok(RWKV_W), tok(RWKV_W),
            pl.BlockSpec((None, TOKEN_TILE, GQA_W), ctx_tile), pl.BlockSpec((None, TOKEN_TILE, GQA_W), lat_tile),
            tok(NA_W), tok(RWKV_W + GQA_W + NA_W),
            pl.BlockSpec((None, None, 3, d), lambda bi, i: (bi, jnp.where(i + skip_tiles < n_ctx_tiles, 0, 1), 0, 0)),
            full(w_out.shape), full((1, RWKV_W)), full((1, RWKV_W)), full((1, d)), full((1, d)), full((2 * RWKV_W, RWKV_W)),
        ],
        out_specs=pl.BlockSpec((None, TOKEN_TILE, d), lambda bi, i: (bi, i, 0)),
        out_shape=jax.ShapeDtypeStruct((b, nt * TOKEN_TILE, d), F32),
        compiler_params=_cparams("parallel", "parallel"),
        name="out",
    )(xc, yf, yb, bonus, ya_ctx, ya_lat, yn, gates, modsel, w_out, gn_g, gn_b, ln_g, ln_b, ones_heads)


def _rope_tables(n_ctx, n_lat):
    t = jnp.arange(n_lat, dtype=jnp.int32)
    row = (t // GRID_W).astype(F32)
    col = (t % GRID_W).astype(F32)
    n_freq = HEAD_DIM // 4
    inv_freq = ROPE_THETA ** (-jnp.arange(n_freq, dtype=F32) / n_freq)
    ang_r = row[:, None] * inv_freq
    ang_c = col[:, None] * inv_freq
    ang = jnp.concatenate([ang_r, ang_r, ang_c, ang_c], axis=-1)
    cos = jnp.concatenate([jnp.ones((n_ctx, HEAD_DIM), F32), jnp.cos(ang)], axis=0)
    sin = jnp.concatenate([jnp.zeros((n_ctx, HEAD_DIM), F32), jnp.sin(ang)], axis=0)
    first = (np.arange(HEAD_DIM) % (HEAD_DIM // 2)) < HEAD_DIM // 4
    sin_a = jnp.where(first, -sin, 0.0)
    sin_b = jnp.where(first, 0.0, sin)
    pair = lambda z: jnp.concatenate([z, z], axis=-1)
    return pair(cos), pair(sin_a), pair(sin_b)


def _block_ones(n_heads):
    blk = np.kron(np.eye(n_heads, dtype=np.float32), np.ones((HEAD_DIM, HEAD_DIM), np.float32))
    return jnp.asarray(blk)


def kernel(x, c, ctx, c_ctx, w_mod, b_mod, w_in, w_out, rwkv_conv, decay_w0, decay_w2, iclr_a0, iclr_a2, rwkv_k_k, rwkv_k_a, rwkv_r_k, rwkv_gn_g, rwkv_gn_b, gqa_q_norm, gqa_k_norm, na_rpb, ln_g, ln_b):
    b, n_lat, d = x.shape
    n_ctx = ctx.shape[1]
    depth = w_mod.shape[0]
    alpha = float((2 * depth) ** 0.25)
    n_ctx_tiles = n_ctx // TOKEN_TILE
    assert n_ctx % TOKEN_TILE == 0 and n_lat % TOKEN_TILE == 0 and n_lat % GRID_W == 0

    n_rows = -(-(b + 1) // SUBLANES) * SUBLANES
    cvec = jnp.zeros((n_rows, d), F32).at[:b].set(c).at[b].set(c_ctx)
    mod = _modulation(cvec, w_mod, b_mod).reshape(depth, n_rows, 3, d)
    mod_ctx = jnp.broadcast_to(mod[:, b][:, None], (depth, b, 3, d))
    modsel = jnp.stack([mod_ctx, mod[:, :b]], axis=2)

    sizes = (RWKV_CONV_W, RWKV_W, GQA_W, GQA_KV_W, GQA_KV_W, GQA_W, NA_W, NA_W, NA_W, NA_W)
    offs = np.concatenate([[0], np.cumsum(sizes)])
    seg = lambda n: w_in[:, :, offs[n]:offs[n + 1]]

    def pair_heads(z, axis):
        shp = z.shape[:axis] + (GQA_KV_HEADS, GQA_Q_HEADS // GQA_KV_HEADS, HEAD_DIM) + z.shape[axis + 1:]
        return jnp.swapaxes(z.reshape(shp), axis, axis + 1).reshape(z.shape)

    w_in_p = jnp.concatenate([seg(0), pair_heads(seg(2), 2), seg(3), seg(4), seg(6), seg(7), seg(8),
                              seg(1), pair_heads(seg(5), 2), seg(9)], axis=2).astype(BF16)
    w_out_p = jnp.concatenate([w_out[:, :RWKV_W], pair_heads(w_out[:, RWKV_W:RWKV_W + GQA_W], 1),
                               w_out[:, RWKV_W + GQA_W:]], axis=1).astype(BF16)

    w = RWKV_W
    zero = jnp.zeros((depth, LOW_RANK, w), F32)
    blocks = (decay_w2[:, 0], decay_w2[:, 1], iclr_a2[:, 0], iclr_a2[:, 1])
    wlr = jnp.concatenate([jnp.concatenate([blk if cc == rr else zero for cc in range(4)], axis=2)
                           for rr, blk in enumerate(blocks)], axis=1)
    wlr_hi = wlr.astype(BF16)
    wlr_lo = (wlr - wlr_hi.astype(F32)).astype(BF16)
    wlr3 = jnp.concatenate([wlr_hi, wlr_hi, wlr_lo], axis=1)
    b0 = jnp.concatenate([decay_w0[:, 0], decay_w0[:, 1], iclr_a0[:, 0], iclr_a0[:, 1]], axis=-1)[:, None, :]

    cos, sin_a, sin_b = _rope_tables(n_ctx, n_lat)
    ones_heads = jnp.tile(_block_ones(RWKV_HEADS), (2, 1)).astype(BF16)
    ones_pair = jnp.tile(_block_ones(2), (2, 1)).astype(BF16)
    pair = lambda z: jnp.concatenate([z, z], axis=-1)[:, None, :]
    qn, kn = pair(gqa_q_norm), pair(gqa_k_norm)

    xc = jnp.concatenate([ctx, x], axis=1)
    t = n_ctx + n_lat
    kv_chunk = next(ch for ch in KV_CHUNKS if t % ch == 0)
    for l in range(depth):
        ru, aq, ak, av, nq4, nk, nv, gates = _projection(xc, modsel[l], w_in_p[l], n_ctx_tiles)
        prep = _rwkv_prep(ru, rwkv_conv[l], wlr3[l], b0[l], rwkv_k_k[l][None], rwkv_k_a[l][None],
                          rwkv_r_k[l].reshape(1, w), ones_heads, n_ctx_tiles)
        yf, yb = _rwkv_scan(prep, n_ctx)
        last = l == depth - 1
        qt, kr, vt4 = _qk_prep(aq, ak, av, cos, sin_a, sin_b, qn[l], kn[l], ones_pair, kv_chunk, n_ctx_tiles)
        k4 = kr.reshape(b, t // kv_chunk, kv_chunk, GQA_KV_W)
        ya = _gqa(qt, k4, vt4, 0, n_lat)
        if last:
            ya_c = ya
        else:
            ya_c = _gqa(qt, kr[:, :n_ctx].reshape(b, 1, n_ctx, GQA_KV_W), vt4[:, :1, :, :n_ctx], n_lat, n_ctx)
        yn = _natten(nq4, nk, nv, _natten_bias(na_rpb[l]), n_ctx)
        xc = _output(xc, yf, yb, prep[9], ya_c, ya, yn, gates, modsel[l], w_out_p[l], rwkv_gn_g[l][None],
                     rwkv_gn_b[l][None], ln_g[l][None], ln_b[l][None], ones_heads, alpha, n_ctx_tiles,
                     n_ctx_tiles if last else 0)
    return xc
```
